```python
import math
import jax
import jax.numpy as jnp
from jax import lax
import numpy as np

D_MODEL = 2048
BATCH = 4
SEQ = 8192
DEPTH = 1

A_HEADS = 8
A_QK_DIM = 64
A_V_DIM = 2 * A_QK_DIM
B_HEADS = 8
B_GROUPS = 2
B_HPG = B_HEADS // B_GROUPS
B_HEAD_DIM = 128
CMP_LEN = 32
CMP_STRIDE = 16
CMP_HIDDEN = 256
SLC_BLOCK = 64
SLC_TOPK = 16
WINDOW = 512
FORCE_SCORE = 1.0e4
ROPE_THETA = 500000.0
ROPE_FRACTION = 4
Q_BLOCK = 128
NSA_Q_BLOCK = 64
EPS = 1e-6
NEG = -1e30

A_Q_W = A_HEADS * 2 * A_QK_DIM
A_V_W = A_HEADS * A_V_DIM
B_Q_W = B_HEADS * B_HEAD_DIM
B_KV_W = B_GROUPS * B_HEAD_DIM
B_GATE_W = B_HEADS * 3
MERGE_W = 2 * D_MODEL
SPLIT_WIDTHS = (A_Q_W, A_Q_W, A_V_W, A_V_W,
                B_Q_W, B_KV_W, B_KV_W, B_KV_W, B_KV_W, B_KV_W, B_KV_W, B_Q_W, B_GATE_W,
                MERGE_W)
IN_WIDTH = sum(SPLIT_WIDTHS)
MIX_WIDTH = A_V_W + B_Q_W

kernel_name = "hybrid_diffattn_nsa_gated_block"


def rms_norm(x, g):
    xf = x.astype(jnp.float32)
    y = xf * lax.rsqrt(jnp.mean(xf * xf, axis=-1, keepdims=True) + EPS)
    return (y * g.astype(jnp.float32)).astype(x.dtype)


def rope_partial(x, pos):
    d = x.shape[-1]
    rd = d // ROPE_FRACTION
    half = rd // 2
    inv = 1.0 / (ROPE_THETA ** (jnp.arange(half, dtype=jnp.float32) * (2.0 / rd)))
    ang = pos.astype(jnp.float32)[:, None] * inv[None, :]
    cos = jnp.cos(ang)[:, None, :]
    sin = jnp.sin(ang)[:, None, :]
    x1 = x[..., :half].astype(jnp.float32)
    x2 = x[..., half:rd].astype(jnp.float32)
    rot = jnp.concatenate([x1 * cos - x2 * sin, x2 * cos + x1 * sin], axis=-1).astype(x.dtype)
    return jnp.concatenate([rot, x[..., rd:]], axis=-1)


def masked_softmax(s, mask):
    s = jnp.where(mask, s.astype(jnp.float32), NEG)
    p = jax.nn.softmax(s, axis=-1)
    return jnp.where(mask, p, 0.0)


def split_cols(p):
    outs = []
    off = 0
    for w in SPLIT_WIDTHS:
        outs.append(p[..., off:off + w])
        off += w
    return outs


def diff_attention(q, k, v, lam, lam_init, sub_g):
    B, S, H, _, dqk = q.shape
    dv = v.shape[-1]
    scale = dqk ** -0.5
    kpos = jnp.arange(S)

    def block(i):
        s0 = i * Q_BLOCK
        qb = lax.dynamic_slice_in_dim(q, s0, Q_BLOCK, axis=1)
        s = jnp.einsum('bqhmd,bkhmd->bhmqk', qb, k).astype(jnp.float32) * scale
        qpos = s0 + jnp.arange(Q_BLOCK)
        mask = kpos[None, :] <= qpos[:, None]
        p = jax.nn.softmax(jnp.where(mask, s, NEG), axis=-1)
        a = p[:, :, 0] - lam * p[:, :, 1]
        return jnp.einsum('bhqk,bkhd->bqhd', a.astype(v.dtype), v)

    o = lax.map(block, jnp.arange(S // Q_BLOCK))
    o = o.transpose(1, 0, 2, 3, 4).reshape(B, S, H, dv)
    return rms_norm(o, sub_g) * (1.0 - lam_init)


def compress_blocks(x, pe, w1, w2):
    B, S, G, d = x.shape
    n_cmp = (S - CMP_LEN) // CMP_STRIDE + 1
    idx = jnp.arange(n_cmp)[:, None] * CMP_STRIDE + jnp.arange(CMP_LEN)[None, :]
    blk = x[:, idx] + pe[None, None, :, None, :]
    flat = blk.transpose(0, 1, 3, 2, 4).reshape(B, n_cmp, G, CMP_LEN * d)
    return jax.nn.silu(flat @ w1) @ w2


def nsa_attention(q, kc_raw, vc_raw, ks, vs, kw, vw, gates, pe_k, pe_v, w1k, w2k, w1v, w2v):
    B, S, H, d = q.shape
    G = B_GROUPS
    scale = d ** -0.5
    dt = q.dtype
    n_cmp = (S - CMP_LEN) // CMP_STRIDE + 1
    cmp_start = jnp.arange(n_cmp) * CMP_STRIDE
    cmp_end = cmp_start + CMP_LEN - 1
    kc = rope_partial(compress_blocks(kc_raw, pe_k, w1k, w2k), cmp_end)
    vc = compress_blocks(vc_raw, pe_v, w1v, w2v)
    n_slc = S // SLC_BLOCK
    n_top = min(SLC_TOPK, n_slc)
    ks_blk = ks.reshape(B, n_slc, SLC_BLOCK, G, d).transpose(0, 3, 1, 2, 4)
    vs_blk = vs.reshape(B, n_slc, SLC_BLOCK, G, d).transpose(0, 3, 1, 2, 4)
    slc_start = jnp.arange(n_slc) * SLC_BLOCK
    overlap = ((cmp_start[:, None] < slc_start[None, :] + SLC_BLOCK) &
               (cmp_start[:, None] + CMP_LEN > slc_start[None, :])).astype(jnp.float32)
    jb = jnp.arange(n_slc)
    b_ix = jnp.arange(B)[:, None, None, None]
    g_ix = jnp.arange(G)[None, :, None, None]
    kw_pad = jnp.pad(kw, ((0, 0), (WINDOW, 0), (0, 0), (0, 0)))
    vw_pad = jnp.pad(vw, ((0, 0), (WINDOW, 0), (0, 0), (0, 0)))
    qg = q.reshape(B, S, G, B_HPG, d)
    gg = gates.reshape(B, S, G, B_HPG, 3)
    QB = NSA_Q_BLOCK

    def block(i):
        s0 = i * QB
        qpos = s0 + jnp.arange(QB)
        qb = lax.dynamic_slice_in_dim(qg, s0, QB, axis=1)
        sc = jnp.einsum('bqghd,bcgd->bghqc', qb, kc) * scale
        pc = masked_softmax(sc, cmp_end[None, :] <= qpos[:, None])
        o_cmp = jnp.einsum('bghqc,bcgd->bqghd', pc.astype(dt), vc)
        imp = jnp.einsum('bghqc,cn->bgqn', pc, overlap)
        cur = qpos // SLC_BLOCK
        forced = (jb[None, :] == 0) | (jb[None, :] == cur[:, None]) | (jb[None, :] == cur[:, None] - 1)
        valid = slc_start[None, :] <= qpos[:, None]
        imp = jnp.where(forced & valid, FORCE_SCORE, imp)
        imp = jnp.where(valid, imp, NEG)
        top_val, top_idx = lax.top_k(imp, n_top)
        sel_ok = top_val > NEG * 0.5
        kg = ks_blk[b_ix, g_ix, top_idx]
        vg = vs_blk[b_ix, g_ix, top_idx].reshape(B, G, QB, n_top * SLC_BLOCK, d)
        ss = jnp.einsum('bqghd,bgqnkd->bghqnk', qb, kg) * scale
        ss = ss.reshape(B, G, B_HPG, QB, n_top * SLC_BLOCK)
        kpos = top_idx[..., None] * SLC_BLOCK + jnp.arange(SLC_BLOCK)
        smask = (kpos <= qpos[None, None, :, None, None]) & sel_ok[..., None]
        ps = masked_softmax(ss, smask.reshape(B, G, 1, QB, n_top * SLC_BLOCK))
        o_slc = jnp.einsum('bghqk,bgqkd->bqghd', ps.astype(dt), vg)
        kwb = lax.dynamic_slice_in_dim(kw_pad, s0, QB + WINDOW, axis=1)
        vwb = lax.dynamic_slice_in_dim(vw_pad, s0, QB + WINDOW, axis=1)
        wpos = s0 - WINDOW + jnp.arange(QB + WINDOW)
        wmask = ((wpos[None, :] <= qpos[:, None]) & (wpos[None, :] > qpos[:, None] - WINDOW)
                 & (wpos[None, :] >= 0))
        sw = jnp.einsum('bqghd,bkgd->bghqk', qb, kwb) * scale
        pw = masked_softmax(sw, wmask)
        o_win = jnp.einsum('bghqk,bkgd->bqghd', pw.astype(dt), vwb)
        gb = lax.dynamic_slice_in_dim(gg, s0, QB, axis=1)
        return gb[..., 0:1] * o_cmp + gb[..., 1:2] * o_slc + gb[..., 2:3] * o_win

    o = lax.map(block, jnp.arange(S // QB))
    return o.transpose(1, 0, 2, 3, 4, 5).reshape(B, S, H, d)


def setup_inputs(seed: int = 0) -> dict:
    key = jax.random.key(seed)
    ks = jax.random.split(key, 22)
    L, D = DEPTH, D_MODEL

    def nrm(k, shape, scale):
        return jax.random.normal(k, shape, jnp.float32) * scale

    return {
        "x": nrm(ks[0], (BATCH, SEQ, D), 1.0),
        "c": nrm(ks[1], (BATCH, D), 1.0),
        "w_ada": nrm(ks[2], (L, D, 3 * D), 0.3 * D ** -0.5),
        "b_ada": nrm(ks[3], (L, 3 * D), 0.01),
        "norm_g": 1.0 + nrm(ks[4], (L, D), 0.02),
        "w_in": nrm(ks[5], (L, D, IN_WIDTH), D ** -0.5),
        "lambda_q1": nrm(ks[6], (L, A_QK_DIM), 0.1),
        "lambda_k1": nrm(ks[7], (L, A_QK_DIM), 0.1),
        "lambda_q2": nrm(ks[8], (L, A_QK_DIM), 0.1),
        "lambda_k2": nrm(ks[9], (L, A_QK_DIM), 0.1),
        "diff_norm_g": 1.0 + nrm(ks[10], (L, A_V_DIM), 0.02),
        "cmp_pe_k": nrm(ks[11], (L, CMP_LEN, B_HEAD_DIM), 0.1),
        "cmp_pe_v": nrm(ks[12], (L, CMP_LEN, B_HEAD_DIM), 0.1),
        "cmp_w1_k": nrm(ks[13], (L, CMP_LEN * B_HEAD_DIM, CMP_HIDDEN), (CMP_LEN * B_HEAD_DIM) ** -0.5),
        "cmp_w2_k": nrm(ks[14], (L, CMP_HIDDEN, B_HEAD_DIM), CMP_HIDDEN ** -0.5),
        "cmp_w1_v": nrm(ks[15], (L, CMP_LEN * B_HEAD_DIM, CMP_HIDDEN), (CMP_LEN * B_HEAD_DIM) ** -0.5),
        "cmp_w2_v": nrm(ks[16], (L, CMP_HIDDEN, B_HEAD_DIM), CMP_HIDDEN ** -0.5),
        "w_branch": nrm(ks[17], (L, MIX_WIDTH, D), A_V_W ** -0.5),
        "w_out": nrm(ks[18], (L, D, D), D ** -0.5),
        "final_norm_g": 1.0 + nrm(ks[19], (D,), 0.02),
    }


def reference(x, c, w_ada, b_ada, norm_g, w_in, lambda_q1, lambda_k1, lambda_q2, lambda_k2,
              diff_norm_g, cmp_pe_k, cmp_pe_v, cmp_w1_k, cmp_w2_k, cmp_w1_v, cmp_w2_v,
              w_branch, w_out, final_norm_g):
    B, S, D = x.shape
    pos = jnp.arange(S)
    for l in range(DEPTH):
        mod = jax.nn.silu(c) @ w_ada[l] + b_ada[l]
        shift, scale, gate = jnp.split(mod, 3, axis=-1)
        h = rms_norm(x, norm_g[l]) * (1.0 + scale[:, None, :]) + shift[:, None, :]
        (aq, ak, av, az, bq, bkc, bvc, bks, bvs, bkw, bvw, bz, bgate, mgate) = split_cols(h @ w_in[l])
        aq = rope_partial(aq.reshape(B, S, A_HEADS * 2, A_QK_DIM), pos).reshape(B, S, A_HEADS, 2, A_QK_DIM)
        ak = rope_partial(ak.reshape(B, S, A_HEADS * 2, A_QK_DIM), pos).reshape(B, S, A_HEADS, 2, A_QK_DIM)
        av = av.reshape(B, S, A_HEADS, A_V_DIM)
        lam_init = 0.8 - 0.6 * math.exp(-0.3 * l)
        lam = (jnp.exp(jnp.sum(lambda_q1[l].astype(jnp.float32) * lambda_k1[l].astype(jnp.float32)))
               - jnp.exp(jnp.sum(lambda_q2[l].astype(jnp.float32) * lambda_k2[l].astype(jnp.float32)))
               + lam_init)
        oa = diff_attention(aq, ak, av, lam, lam_init, diff_norm_g[l])
        ya = (oa.reshape(B, S, A_V_W) * jax.nn.silu(az)) @ w_branch[l, :A_V_W]
        bq = rope_partial(bq.reshape(B, S, B_HEADS, B_HEAD_DIM), pos)
        kv_shape = (B, S, B_GROUPS, B_HEAD_DIM)
        bks = rope_partial(bks.reshape(kv_shape), pos)
        bkw = rope_partial(bkw.reshape(kv_shape), pos)
        bg = jax.nn.sigmoid(bgate).reshape(B, S, B_HEADS, 3)
        ob = nsa_attention(bq, bkc.reshape(kv_shape), bvc.reshape(kv_shape), bks, bvs.reshape(kv_shape),
                           bkw, bvw.reshape(kv_shape), bg, cmp_pe_k[l], cmp_pe_v[l],
                           cmp_w1_k[l], cmp_w2_k[l], cmp_w1_v[l], cmp_w2_v[l])
        yb = (ob.reshape(B, S, B_Q_W) * jax.nn.silu(bz)) @ w_branch[l, A_V_W:]
        ga, gb = jnp.split(jax.nn.sigmoid(mgate), 2, axis=-1)
        y = (ga * ya + gb * yb) @ w_out[l]
        x = x + gate[:, None, :] * y
    return rms_norm(x, final_norm_g)
```

```python
import functools
import math

import jax
import jax.numpy as jnp
from jax import lax
from jax.experimental import pallas as pl
from jax.experimental.pallas import tpu as pltpu

F32 = jnp.float32
BF16 = jnp.bfloat16

D_MODEL = 2048
A_HEADS = 8
A_QK_DIM = 64
A_V_DIM = 128
B_HEADS = 8
B_GROUPS = 2
B_HPG = B_HEADS // B_GROUPS
B_HEAD_DIM = 128
CMP_LEN = 32
CMP_STRIDE = 16
CMP_HIDDEN = 256
SLC_BLOCK = 64
SLC_TOPK = 16
WINDOW = 512
FORCE_SCORE = 1.0e4
ROPE_THETA = 500000.0
ROPE_FRACTION = 4
EPS = 1e-6
NEG = -1e30
LAM_INIT = 0.8 - 0.6 * math.exp(-0.3 * 0)

LANE = 128
VMEM_LIMIT = 56 * 1024 * 1024

CB_AQ, CB_AK, CB_BQ, CB_BKS, CB_BKW = 0, 8, 16, 24, 26
CB_AV, CB_AZ, CB_BKC, CB_BVC, CB_BVS, CB_BVW, CB_BZ = 28, 36, 44, 46, 48, 50, 52
CB_BGATE, CB_GA, CB_GB = 60, 64, 80
P_WIDTH = 96 * LANE
PROJ_TN = 512
N_ROPE_A_TILES = (CB_BQ * LANE) // PROJ_TN
N_ROPE_TILES = (CB_AV * LANE) // PROJ_TN


def _dot(a, b):
    return jnp.dot(a, b, preferred_element_type=F32)


def _dot_nt(a, b):
    return lax.dot_general(a, b, (((1,), (1,)), ((), ())), preferred_element_type=F32)


def _silu(v):
    return v * jax.nn.sigmoid(v)


def _ada_kernel(c_ref, w_ref, b_ref, o_ref):
    o_ref[...] = _dot(_silu(c_ref[...]), w_ref[...]) + b_ref[...]


def _ada_call(c_pad, w, b):
    rows, d = c_pad.shape
    n = w.shape[1]
    tn = 768
    return pl.pallas_call(
        _ada_kernel,
        grid=(n // tn,),
        in_specs=[pl.BlockSpec((rows, d), lambda j: (0, 0)),
                  pl.BlockSpec((d, tn), lambda j: (0, j)),
                  pl.BlockSpec((1, tn), lambda j: (0, j))],
        out_specs=pl.BlockSpec((rows, tn), lambda j: (0, j)),
        out_shape=jax.ShapeDtypeStruct((rows, n), F32),
        compiler_params=pltpu.CompilerParams(vmem_limit_bytes=VMEM_LIMIT),
        name="ada",
    )(c_pad, w, b)


def _rope_store(acc, o_ref, c_ref, s1_ref, s2_ref, half):
    for g in range(acc.shape[1] // LANE):
        a = acc[:, g * LANE:(g + 1) * LANE]
        up = pltpu.roll(a, LANE - half, axis=1)
        dn = pltpu.roll(a, half, axis=1)
        o_ref[:, g * LANE:(g + 1) * LANE] = (
            a * c_ref[...] + up * s1_ref[...] + dn * s2_ref[...]).astype(o_ref.dtype)


def _inproj_kernel(x_ref, g_ref, sc_ref, sh_ref, w_ref, ca_ref, s1a_ref, s2a_ref,
                   cb_ref, s1b_ref, s2b_ref, o_ref, h_ref):
    j = pl.program_id(1)

    @pl.when(j == 0)
    def _():
        x = x_ref[...]
        ms = jnp.mean(x * x, axis=-1, keepdims=True)
        xn = x * lax.rsqrt(ms + EPS)
        h_ref[...] = (xn * (g_ref[...] * (1.0 + sc_ref[...])) + sh_ref[...]).astype(h_ref.dtype)

    acc = _dot(h_ref[...], w_ref[...])

    @pl.when(j < N_ROPE_A_TILES)
    def _():
        _rope_store(acc, o_ref, ca_ref, s1a_ref, s2a_ref, A_QK_DIM // ROPE_FRACTION // 2)

    @pl.when((j >= N_ROPE_A_TILES) & (j < N_ROPE_TILES))
    def _():
        _rope_store(acc, o_ref, cb_ref, s1b_ref, s2b_ref, B_HEAD_DIM // ROPE_FRACTION // 2)

    @pl.when(j >= N_ROPE_TILES)
    def _():
        o_ref[...] = acc.astype(o_ref.dtype)


def _inproj_call(x2, g, scale, shift, w, tabs_a, tabs_b, seq):
    t, d = x2.shape
    tm = min(1024, seq)
    per_seq = seq // tm
    tab = pl.BlockSpec((tm, LANE), lambda i, j: (i % per_seq, 0))
    mod = pl.BlockSpec((None, 1, d), lambda i, j: (i // per_seq, 0, 0))
    return pl.pallas_call(
        _inproj_kernel,
        grid=(t // tm, P_WIDTH // PROJ_TN),
        in_specs=[pl.BlockSpec((tm, d), lambda i, j: (i, 0)),
                  pl.BlockSpec((1, d), lambda i, j: (0, 0)),
                  mod, mod,
                  pl.BlockSpec((d, PROJ_TN), lambda i, j: (0, j)),
                  tab, tab, tab, tab, tab, tab],
        out_specs=pl.BlockSpec((tm, PROJ_TN), lambda i, j: (i, j)),
        out_shape=jax.ShapeDtypeStruct((t, P_WIDTH), BF16),
        scratch_shapes=[pltpu.VMEM((tm, d), BF16)],
        compiler_params=pltpu.CompilerParams(
            dimension_semantics=("arbitrary", "arbitrary"), vmem_limit_bytes=VMEM_LIMIT),
        name="inproj",
    )(x2, g, scale, shift, w, *tabs_a, *tabs_b)


def _flash_init(m_ref, l_ref, acc_ref):
    m_ref[...] = jnp.full(m_ref.shape, NEG, F32)
    l_ref[...] = jnp.zeros(l_ref.shape, F32)
    acc_ref[...] = jnp.zeros(acc_ref.shape, F32)


def _flash_update(s, v, m_ref, l_ref, acc_ref):
    m_old = m_ref[...]
    m_new = jnp.maximum(m_old, jnp.max(s, axis=-1, keepdims=True))
    alpha = jnp.exp(m_old - m_new)
    p = jnp.exp(s - m_new)
    l_ref[...] = alpha * l_ref[...] + jnp.sum(p, axis=-1, keepdims=True)
    acc_ref[...] = alpha * acc_ref[...] + _dot(p.astype(v.dtype), v)
    m_ref[...] = m_new


def _diff_kernel(q_ref, k_ref, v_ref, z_ref, lq1_ref, lk1_ref, lq2_ref, lk2_ref, g_ref, o_ref,
                 qs_ref, m_ref, l_ref, acc_ref, *, tq):
    i = pl.program_id(2)
    q = q_ref[...]
    lane = lax.broadcasted_iota(jnp.int32, q.shape, 1)
    zero = jnp.zeros_like(q)
    qs_ref[0:tq, :] = jnp.where(lane < A_QK_DIM, q, zero)
    qs_ref[tq:2 * tq, :] = jnp.where(lane >= A_QK_DIM, q, zero)
    _flash_init(m_ref, l_ref, acc_ref)

    def step(j, causal):
        k0 = pl.multiple_of(j * tq, tq)
        s = _dot_nt(qs_ref[...], k_ref[pl.ds(k0, tq), :])
        if causal:
            row = lax.broadcasted_iota(jnp.int32, s.shape, 0)
            col = lax.broadcasted_iota(jnp.int32, s.shape, 1)
            qrow = jnp.where(row >= tq, row - tq, row)
            s = jnp.where(col <= qrow, s, NEG)
        _flash_update(s, v_ref[pl.ds(k0, tq), :], m_ref, l_ref, acc_ref)

    def body(j, carry):
        step(j, False)
        return carry

    lax.fori_loop(0, i, body, 0)
    step(i, True)

    o = acc_ref[...] / l_ref[...]
    lam = (jnp.exp(jnp.sum(lq1_ref[...] * lk1_ref[...], axis=-1, keepdims=True))
           - jnp.exp(jnp.sum(lq2_ref[...] * lk2_ref[...], axis=-1, keepdims=True)) + LAM_INIT)
    d = o[0:tq] - lam * o[tq:2 * tq]
    dn = d * lax.rsqrt(jnp.mean(d * d, axis=-1, keepdims=True) + EPS)
    dn = dn * g_ref[...] * (1.0 - LAM_INIT)
    o_ref[...] = (dn * _silu(z_ref[...].astype(F32))).astype(o_ref.dtype)


def _diff_call(p3, lq1, lk1, lq2, lk2, sub_g):
    b, s, _ = p3.shape
    tq = 256
    lam_spec = pl.BlockSpec((1, A_QK_DIM), lambda bi, h, i: (0, 0))
    return pl.pallas_call(
        functools.partial(_diff_kernel, tq=tq),
        grid=(b, A_HEADS, s // tq),
        in_specs=[pl.BlockSpec((None, tq, LANE), lambda bi, h, i: (bi, i, CB_AQ + h)),
                  pl.BlockSpec((None, s, LANE), lambda bi, h, i: (bi, 0, CB_AK + h)),
                  pl.BlockSpec((None, s, LANE), lambda bi, h, i: (bi, 0, CB_AV + h)),
                  pl.BlockSpec((None, tq, LANE), lambda bi, h, i: (bi, i, CB_AZ + h)),
                  lam_spec, lam_spec, lam_spec, lam_spec,
                  pl.BlockSpec((1, A_V_DIM), lambda bi, h, i: (0, 0))],
        out_specs=pl.BlockSpec((None, tq, LANE), lambda bi, h, i: (bi, i, h)),
        out_shape=jax.ShapeDtypeStruct((b, s, A_HEADS * A_V_DIM), BF16),
        scratch_shapes=[pltpu.VMEM((2 * tq, LANE), BF16),
                        pltpu.VMEM((2 * tq, 1), F32),
                        pltpu.VMEM((2 * tq, 1), F32),
                        pltpu.VMEM((2 * tq, A_V_DIM), F32)],
        compiler_params=pltpu.CompilerParams(
            dimension_semantics=("arbitrary", "arbitrary", "arbitrary"), vmem_limit_bytes=VMEM_LIMIT),
        name="diff_attn",
    )(p3, p3, p3, p3, lq1, lk1, lq2, lk2, sub_g)


def _compress_kernel(xk_ref, xv_ref, pek_ref, pev_ref, w1k_ref, w2k_ref, w1v_ref, w2v_ref,
                     c_ref, s1_ref, s2_ref, kc_ref, vc_ref):
    half_w = xk_ref.shape[1]
    n = xk_ref.shape[0]

    def mlp(x_ref, pe_ref, w1_ref, w2_ref):
        x = x_ref[...].astype(F32)
        top = _dot((x + pe_ref[0:1, :]).astype(BF16), w1_ref[0:half_w, :])
        bot = _dot((x + pe_ref[1:2, :]).astype(BF16), w1_ref[half_w:2 * half_w, :])
        hid = top + pltpu.roll(bot, n - 1, axis=0)
        return _dot(_silu(hid).astype(BF16), w2_ref[...])

    kc = mlp(xk_ref, pek_ref, w1k_ref, w2k_ref)
    half = B_HEAD_DIM // ROPE_FRACTION // 2
    up = pltpu.roll(kc, LANE - half, axis=1)
    dn = pltpu.roll(kc, half, axis=1)
    kc_ref[...] = (kc * c_ref[...] + up * s1_ref[...] + dn * s2_ref[...]).astype(kc_ref.dtype)
    vc_ref[...] = mlp(xv_ref, pev_ref, w1v_ref, w2v_ref).astype(vc_ref.dtype)


def _compress_call(xk, xv, pek, pev, w1k, w2k, w1v, w2v, tabs):
    b, g, n, hw = xk.shape
    x_spec = pl.BlockSpec((None, None, n, hw), lambda bi, gi: (bi, gi, 0, 0))
    o_spec = pl.BlockSpec((None, None, n, B_HEAD_DIM), lambda bi, gi: (bi, gi, 0, 0))

    def full(a):
        return pl.BlockSpec(a.shape, lambda bi, gi: (0,) * a.ndim)

    consts = (pek, pev, w1k, w2k, w1v, w2v, *tabs)
    return pl.pallas_call(
        _compress_kernel,
        grid=(b, g),
        in_specs=[x_spec, x_spec] + [full(a) for a in consts],
        out_specs=[o_spec, o_spec],
        out_shape=[jax.ShapeDtypeStruct((b, g, n, B_HEAD_DIM), BF16)] * 2,
        compiler_params=pltpu.CompilerParams(
            dimension_semantics=("arbitrary", "arbitrary"), vmem_limit_bytes=VMEM_LIMIT),
        name="compress",
    )(xk, xv, *consts)


def _masked_softmax(s, mask):
    s = jnp.where(mask, s, NEG)
    m = jnp.max(s, axis=-1, keepdims=True)
    p = jnp.where(mask, jnp.exp(s - m), 0.0)
    l = jnp.sum(p, axis=-1, keepdims=True)
    return p * jnp.where(l > 0.0, 1.0 / l, 0.0)


def _nsa_kernel(q_ref, kc_ref, vc_ref, ks_ref, vs_ref, kw_ref, vw_ref, e_ref, ov_ref, gate_ref, z_ref,
                o_ref, qa_ref, m_ref, l_ref, acc_ref, *, tq, tk, n_slc, n_top):
    i = pl.program_id(2)
    s0 = i * tq
    hd = B_HEAD_DIM
    rows = B_HPG * tq
    qs = jnp.concatenate([q_ref[:, h * hd:(h + 1) * hd] for h in range(B_HPG)], axis=0)
    qpos = s0 + lax.broadcasted_iota(jnp.int32, (tq, 1), 0)
    rid = lax.broadcasted_iota(jnp.int32, (rows, 1), 0)
    qpos_r = s0 + (rid & (tq - 1))

    n_cmp_pad = kc_ref.shape[0]
    cend = lax.broadcasted_iota(jnp.int32, (1, n_cmp_pad), 1) * CMP_STRIDE + (CMP_LEN - 1)
    pc = _masked_softmax(_dot_nt(qs, kc_ref[...]), cend <= qpos_r)
    o_cmp = _dot(pc.astype(BF16), vc_ref[...])
    psum = pc[0:tq]
    for h in range(1, B_HPG):
        psum = psum + pc[h * tq:(h + 1) * tq]
    imp = _dot(psum.astype(BF16), ov_ref[...])

    jb = lax.broadcasted_iota(jnp.int32, (tq, LANE), 1)
    jbf = jb.astype(F32)
    cur = qpos // SLC_BLOCK
    valid = (jb * SLC_BLOCK <= qpos) & (jb < n_slc)
    forced = (jb == 0) | (jb == cur) | (jb == cur - 1)
    imp = jnp.where(forced & valid, FORCE_SCORE, imp)
    imp = jnp.where(valid, imp, NEG)
    selb = jnp.full((tq, LANE), NEG, F32)
    for _ in range(n_top):
        mx = jnp.max(imp, axis=-1, keepdims=True)
        idx = jnp.min(jnp.where(imp == mx, jbf, float(LANE)), axis=-1, keepdims=True)
        hit = jbf == idx
        selb = jnp.where(hit & (mx > NEG * 0.5), 0.0, selb)
        imp = jnp.where(hit, -3.0e38, imp)

    selb16 = selb.astype(BF16)
    for h in range(B_HPG):
        qa_ref[h * tq:(h + 1) * tq, 0:hd] = q_ref[:, h * hd:(h + 1) * hd]
        qa_ref[h * tq:(h + 1) * tq, hd:2 * hd] = selb16
    _flash_init(m_ref, l_ref, acc_ref)

    def slc_step(j, causal):
        k0 = pl.multiple_of(j * tk, tk)
        ka = jnp.concatenate([ks_ref[pl.ds(k0, tk), :], e_ref[pl.ds(k0, tk), :]], axis=1)
        s = _dot_nt(qa_ref[...], ka)
        if causal:
            kpos = k0 + lax.broadcasted_iota(jnp.int32, (1, tk), 1)
            s = jnp.where(kpos <= qpos_r, s, NEG)
        _flash_update(s, vs_ref[pl.ds(k0, tk), :], m_ref, l_ref, acc_ref)

    def body(j, carry):
        slc_step(j, False)
        return carry

    jd = s0 // tk
    lax.fori_loop(0, jd, body, 0)
    slc_step(jd, True)
    o_slc = acc_ref[...] / l_ref[...]

    wlen = WINDOW + tq
    w0 = pl.multiple_of(jnp.maximum(s0 - WINDOW, 0), tq)
    wpos = w0 + lax.broadcasted_iota(jnp.int32, (1, wlen), 1)
    wmask = (wpos <= qpos_r) & (wpos > qpos_r - WINDOW)
    pw = _masked_softmax(_dot_nt(qs, kw_ref[pl.ds(w0, wlen), :]), wmask)
    o_win = _dot(pw.astype(BF16), vw_ref[pl.ds(w0, wlen), :])

    gates = jax.nn.sigmoid(gate_ref[...].astype(F32))
    for h in range(B_HPG):
        r = slice(h * tq, (h + 1) * tq)
        o = (gates[:, 3 * h:3 * h + 1] * o_cmp[r] + gates[:, 3 * h + 1:3 * h + 2] * o_slc[r]
             + gates[:, 3 * h + 2:3 * h + 3] * o_win[r])
        z = z_ref[:, h * hd:(h + 1) * hd].astype(F32)
        o_ref[:, h * hd:(h + 1) * hd] = (o * _silu(z)).astype(o_ref.dtype)


def _nsa_call(p3, kc, vc, e_mat, ov_mat):
    b, s, _ = p3.shape
    tq, tk = 128, 512
    n_slc = s // SLC_BLOCK
    n_top = min(SLC_TOPK, n_slc)
    gw = B_HPG * B_HEAD_DIM
    rows = B_HPG * tq
    n_cmp_pad = kc.shape[2]

    def slab(cb):
        return pl.BlockSpec((None, s, LANE), lambda bi, g, i: (bi, 0, cb + g))

    cmp_spec = pl.BlockSpec((None, None, n_cmp_pad, B_HEAD_DIM), lambda bi, g, i: (bi, g, 0, 0))
    return pl.pallas_call(
        functools.partial(_nsa_kernel, tq=tq, tk=tk, n_slc=n_slc, n_top=n_top),
        grid=(b, B_GROUPS, s // tq),
        in_specs=[pl.BlockSpec((None, tq, gw), lambda bi, g, i: (bi, i, CB_BQ * LANE // gw + g)),
                  cmp_spec, cmp_spec,
                  slab(CB_BKS), slab(CB_BVS), slab(CB_BKW), slab(CB_BVW),
                  pl.BlockSpec((s, LANE), lambda bi, g, i: (0, 0)),
                  pl.BlockSpec((n_cmp_pad, LANE), lambda bi, g, i: (0, 0)),
                  pl.BlockSpec((None, tq, LANE), lambda bi, g, i: (bi, i, CB_BGATE + g)),
                  pl.BlockSpec((None, tq, gw), lambda bi, g, i: (bi, i, CB_BZ * LANE // gw + g))],
        out_specs=pl.BlockSpec((None, tq, gw), lambda bi, g, i: (bi, i, g)),
        out_shape=jax.ShapeDtypeStruct((b, s, B_HEADS * B_HEAD_DIM), BF16),
        scratch_shapes=[pltpu.VMEM((rows, 2 * B_HEAD_DIM), BF16),
                        pltpu.VMEM((rows, 1), F32),
                        pltpu.VMEM((rows, 1), F32),
                        pltpu.VMEM((rows, B_HEAD_DIM), F32)],
        compiler_params=pltpu.CompilerParams(
            dimension_semantics=("arbitrary", "arbitrary", "arbitrary"), vmem_limit_bytes=VMEM_LIMIT),
        name="nsa",
    )(p3, kc, vc, p3, p3, p3, p3, e_mat, ov_mat, p3, p3)


def _out_kernel(oa_ref, ob_ref, ga_ref, gb_ref, x_ref, gate_ref, wa_ref, wb_ref, wo_ref, fg_ref, o_ref):
    ya = _dot(oa_ref[...], wa_ref[...])
    yb = _dot(ob_ref[...], wb_ref[...])
    mix = (jax.nn.sigmoid(ga_ref[...].astype(F32)) * ya
           + jax.nn.sigmoid(gb_ref[...].astype(F32)) * yb)
    y = _dot(mix.astype(BF16), wo_ref[...])
    xo = x_ref[...] + gate_ref[...] * y
    o_ref[...] = xo * lax.rsqrt(jnp.mean(xo * xo, axis=-1, keepdims=True) + EPS) * fg_ref[...]


def _out_call(oa, ob, p2, x2, gate, wa, wb, wo, fg, seq):
    t, d = x2.shape
    tm = 256
    per_seq = seq // tm
    aw = oa.shape[1]
    bw = ob.shape[1]

    def resident(a):
        return pl.BlockSpec(a.shape, lambda i: (0, 0), pipeline_mode=pl.Buffered(1))

    return pl.pallas_call(
        _out_kernel,
        grid=(t // tm,),
        in_specs=[pl.BlockSpec((tm, aw), lambda i: (i, 0)),
                  pl.BlockSpec((tm, bw), lambda i: (i, 0)),
                  pl.BlockSpec((tm, d), lambda i: (i, CB_GA * LANE // d)),
                  pl.BlockSpec((tm, d), lambda i: (i, CB_GB * LANE // d)),
                  pl.BlockSpec((tm, d), lambda i: (i, 0)),
                  pl.BlockSpec((None, 1, d), lambda i: (i // per_seq, 0, 0)),
                  resident(wa), resident(wb), resident(wo),
                  pl.BlockSpec((1, d), lambda i: (0, 0))],
        out_specs=pl.BlockSpec((tm, d), lambda i: (i, 0)),
        out_shape=jax.ShapeDtypeStruct((t, d), F32),
        compiler_params=pltpu.CompilerParams(
            dimension_semantics=("arbitrary",), vmem_limit_bytes=VMEM_LIMIT),
        name="out_proj",
    )(oa, ob, p2, p2, x2, gate, wa, wb, wo, fg)


def _rope_tables(pos, head_dim):
    rd = head_dim // ROPE_FRACTION
    half = rd // 2
    inv = 1.0 / (ROPE_THETA ** (jnp.arange(half, dtype=F32) * (2.0 / rd)))
    ang = pos.astype(F32)[:, None] * inv[None, :]
    cos, sin = jnp.cos(ang), jnp.sin(ang)
    n = pos.shape[0]
    pad = jnp.zeros((n, head_dim - rd), F32)
    zero = jnp.zeros((n, half), F32)
    c = jnp.concatenate([cos, cos, pad + 1.0], axis=1)
    s1 = jnp.concatenate([-sin, zero, pad], axis=1)
    s2 = jnp.concatenate([zero, sin, pad], axis=1)
    reps = LANE // head_dim
    return tuple(jnp.tile(a, (1, reps)) for a in (c, s1, s2))


def _rearranged_w_in(w):
    off = {}
    o = 0
    widths = (("aq", 1024), ("ak", 1024), ("av", 1024), ("az", 1024), ("bq", 1024), ("bkc", 256),
              ("bvc", 256), ("bks", 256), ("bvs", 256), ("bkw", 256), ("bvw", 256), ("bz", 1024),
              ("bgate", 24), ("mgate", 4096))
    for name, wd in widths:
        off[name] = (o, wd)
        o += wd

    def col(name, scale=None):
        a, wd = off[name]
        c = w[:, a:a + wd]
        return c if scale is None else c * scale

    d = w.shape[0]
    gpg = B_HPG * 3
    a_g, _ = off["bgate"]
    zpad = jnp.zeros((d, LANE - gpg), w.dtype)
    gate_cols = [w[:, a_g:a_g + gpg], zpad, w[:, a_g + gpg:a_g + 2 * gpg], zpad,
                 jnp.zeros((d, 2 * LANE), w.dtype)]
    parts = [col("aq", A_QK_DIM ** -0.5), col("ak"), col("bq", B_HEAD_DIM ** -0.5), col("bks"), col("bkw"),
             col("av"), col("az"), col("bkc"), col("bvc"), col("bvs"), col("bvw"), col("bz"),
             *gate_cols, col("mgate")]
    return jnp.concatenate(parts, axis=1).astype(BF16)


def kernel(x, c, w_ada, b_ada, norm_g, w_in, lambda_q1, lambda_k1, lambda_q2, lambda_k2, diff_norm_g,
           cmp_pe_k, cmp_pe_v, cmp_w1_k, cmp_w2_k, cmp_w1_v, cmp_w2_v, w_branch, w_out, final_norm_g):
    b, s, d = x.shape
    assert d == D_MODEL and s % 1024 == 0 and s // SLC_BLOCK <= LANE
    t = b * s
    x2 = x.reshape(t, d)

    c_pad = jnp.pad(c, ((0, 8 - b % 8 if b % 8 else 0), (0, 0)))
    mod = _ada_call(c_pad, w_ada[0], b_ada[0][None, :])[:b]
    shift = mod[:, None, 0:d]
    scale = mod[:, None, d:2 * d]
    gate = mod[:, None, 2 * d:3 * d]

    pos = jnp.arange(s)
    p2 = _inproj_call(x2, norm_g[0][None, :], scale, shift, _rearranged_w_in(w_in[0]),
                      _rope_tables(pos, A_QK_DIM), _rope_tables(pos, B_HEAD_DIM), s)
    p3 = p2.reshape(b, s, P_WIDTH)

    oa = _diff_call(p3, lambda_q1, lambda_k1, lambda_q2, lambda_k2, diff_norm_g)

    n_str = s // CMP_STRIDE
    hw = CMP_STRIDE * B_HEAD_DIM

    def strides(cb):
        a = p3[:, :, cb * LANE:(cb + B_GROUPS) * LANE].reshape(b, n_str, CMP_STRIDE, B_GROUPS, B_HEAD_DIM)
        return a.transpose(0, 3, 1, 2, 4).reshape(b, B_GROUPS, n_str, hw)

    cmp_end = jnp.arange(n_str) * CMP_STRIDE + (CMP_LEN - 1)
    kc, vc = _compress_call(
        strides(CB_BKC), strides(CB_BVC),
        cmp_pe_k[0].reshape(2, hw), cmp_pe_v[0].reshape(2, hw),
        cmp_w1_k[0].astype(BF16), cmp_w2_k[0].astype(BF16),
        cmp_w1_v[0].astype(BF16), cmp_w2_v[0].astype(BF16),
        _rope_tables(cmp_end, B_HEAD_DIM))
    e_mat = (jnp.arange(s)[:, None] // SLC_BLOCK == jnp.arange(LANE)[None, :]).astype(BF16)
    cmp_start = jnp.arange(n_str) * CMP_STRIDE
    slc_start = jnp.arange(LANE) * SLC_BLOCK
    ov_mat = ((cmp_start[:, None] < slc_start[None, :] + SLC_BLOCK)
              & (cmp_start[:, None] + CMP_LEN > slc_start[None, :])).astype(BF16)
    ob = _nsa_call(p3, kc, vc, e_mat, ov_mat)

    wbr = w_branch[0].astype(BF16)
    a_w = A_HEADS * A_V_DIM
    out = _out_call(oa.reshape(t, a_w), ob.reshape(t, B_HEADS * B_HEAD_DIM), p2, x2, gate,
                    wbr[:a_w], wbr[a_w:], w_out[0].astype(BF16), final_norm_g[None, :], s)
    return out.reshape(b, s, d)
```

```python
import functools
import math

import jax
import jax.numpy as jnp
from jax import lax
from jax.experimental import pallas as pl
from jax.experimental.pallas import tpu as pltpu

F32 = jnp.float32
BF16 = jnp.bfloat16

D_MODEL = 2048
A_HEADS = 8
A_QK_DIM = 64
A_V_DIM = 128
B_HEADS = 8
B_GROUPS = 2
B_HPG = B_HEADS // B_GROUPS
B_HEAD_DIM = 128
CMP_LEN = 32
CMP_STRIDE = 16
CMP_HIDDEN = 256
SLC_BLOCK = 64
SLC_TOPK = 16
WINDOW = 512
FORCE_SCORE = 1.0e4
ROPE_THETA = 500000.0
ROPE_FRACTION = 4
EPS = 1e-6
NEG = -1e30
LAM_INIT = 0.8 - 0.6 * math.exp(-0.3 * 0)
LOG2E = math.log2(math.e)

LANE = 128
VMEM_LIMIT = 56 * 1024 * 1024

CB_AQ, CB_AK, CB_BQ, CB_BKS, CB_BKW = 0, 8, 16, 24, 26
CB_AV, CB_AZ, CB_BKC, CB_BVC, CB_BVS, CB_BVW, CB_BZ = 28, 36, 44, 46, 48, 50, 52
CB_BGATE, CB_GA, CB_GB = 60, 64, 80
P_WIDTH = 96 * LANE
PROJ_TN = 512
N_ROPE_A_TILES = (CB_BQ * LANE) // PROJ_TN
N_ROPE_TILES = (CB_AV * LANE) // PROJ_TN


def _dot(a, b):
    return jnp.dot(a, b, preferred_element_type=F32)


def _dot_nt(a, b):
    return lax.dot_general(a, b, (((1,), (1,)), ((), ())), preferred_element_type=F32)


def _silu(v):
    return v * jax.nn.sigmoid(v)


def _ada_kernel(c_ref, w_ref, b_ref, o_ref):
    o_ref[...] = _dot(_silu(c_ref[...]), w_ref[...]) + b_ref[...]


def _ada_call(c_pad, w, b):
    rows, d = c_pad.shape
    n = w.shape[1]
    tn = 768
    return pl.pallas_call(
        _ada_kernel,
        grid=(n // tn,),
        in_specs=[pl.BlockSpec((rows, d), lambda j: (0, 0)),
                  pl.BlockSpec((d, tn), lambda j: (0, j)),
                  pl.BlockSpec((1, tn), lambda j: (0, j))],
        out_specs=pl.BlockSpec((rows, tn), lambda j: (0, j)),
        out_shape=jax.ShapeDtypeStruct((rows, n), F32),
        compiler_params=pltpu.CompilerParams(vmem_limit_bytes=VMEM_LIMIT),
        name="ada",
    )(c_pad, w, b)


def _rope_store(acc, o_ref, c_ref, s1_ref, s2_ref, half):
    for g in range(acc.shape[1] // LANE):
        a = acc[:, g * LANE:(g + 1) * LANE]
        up = pltpu.roll(a, LANE - half, axis=1)
        dn = pltpu.roll(a, half, axis=1)
        o_ref[:, g * LANE:(g + 1) * LANE] = (
            a * c_ref[...] + up * s1_ref[...] + dn * s2_ref[...]).astype(o_ref.dtype)


def _inproj_kernel(x_ref, g_ref, sc_ref, sh_ref, w_ref, ca_ref, s1a_ref, s2a_ref,
                   cb_ref, s1b_ref, s2b_ref, o_ref, h_ref):
    j = pl.program_id(1)

    @pl.when(j == 0)
    def _():
        x = x_ref[...]
        ms = jnp.mean(x * x, axis=-1, keepdims=True)
        xn = x * lax.rsqrt(ms + EPS)
        h_ref[...] = (xn * (g_ref[...] * (1.0 + sc_ref[...])) + sh_ref[...]).astype(h_ref.dtype)

    acc = _dot(h_ref[...], w_ref[...])

    @pl.when(j < N_ROPE_A_TILES)
    def _():
        _rope_store(acc, o_ref, ca_ref, s1a_ref, s2a_ref, A_QK_DIM // ROPE_FRACTION // 2)

    @pl.when((j >= N_ROPE_A_TILES) & (j < N_ROPE_TILES))
    def _():
        _rope_store(acc, o_ref, cb_ref, s1b_ref, s2b_ref, B_HEAD_DIM // ROPE_FRACTION // 2)

    @pl.when(j >= N_ROPE_TILES)
    def _():
        o_ref[...] = acc.astype(o_ref.dtype)


def _inproj_call(x2, g, scale, shift, w, tabs_a, tabs_b, seq):
    t, d = x2.shape
    tm = min(1024, seq)
    per_seq = seq // tm
    tab = pl.BlockSpec((tm, LANE), lambda i, j: (i % per_seq, 0))
    mod = pl.BlockSpec((None, 1, d), lambda i, j: (i // per_seq, 0, 0))
    return pl.pallas_call(
        _inproj_kernel,
        grid=(t // tm, P_WIDTH // PROJ_TN),
        in_specs=[pl.BlockSpec((tm, d), lambda i, j: (i, 0)),
                  pl.BlockSpec((1, d), lambda i, j: (0, 0)),
                  mod, mod,
                  pl.BlockSpec((d, PROJ_TN), lambda i, j: (0, j)),
                  tab, tab, tab, tab, tab, tab],
        out_specs=pl.BlockSpec((tm, PROJ_TN), lambda i, j: (i, j)),
        out_shape=jax.ShapeDtypeStruct((t, P_WIDTH), BF16),
        scratch_shapes=[pltpu.VMEM((tm, d), BF16)],
        compiler_params=pltpu.CompilerParams(
            dimension_semantics=("arbitrary", "arbitrary"), vmem_limit_bytes=VMEM_LIMIT),
        name="inproj",
    )(x2, g, scale, shift, w, *tabs_a, *tabs_b)


def _flash_init(m_ref, acc_ref):
    m_ref[...] = jnp.full(m_ref.shape, NEG, F32)
    acc_ref[...] = jnp.zeros(acc_ref.shape, F32)


def _with_ones(v):
    return jnp.concatenate([v, jnp.ones(v.shape, v.dtype)], axis=1)


def _flash_update(s, v, m_ref, acc_ref):
    m_old = m_ref[...]
    m_new = jnp.maximum(m_old, jnp.max(s, axis=-1, keepdims=True))
    alpha = jnp.exp2(m_old - m_new)
    p = jnp.exp2(s - jnp.tile(m_new, (1, s.shape[1] // LANE)))
    acc_ref[...] = jnp.tile(alpha, (1, 2)) * acc_ref[...] + _dot(p.astype(v.dtype), _with_ones(v))
    m_ref[...] = m_new


def _flash_result(acc_ref):
    acc = acc_ref[...]
    return acc[:, 0:LANE] / acc[:, LANE:2 * LANE]


def _masked_attend(s, mask, v):
    s = jnp.where(mask, s, NEG)
    p = jnp.exp2(s - jnp.max(s, axis=-1, keepdims=True))
    pv = _dot(p.astype(v.dtype), _with_ones(v))
    return pv[:, 0:LANE] / pv[:, LANE:2 * LANE]


def _diff_kernel(q_ref, k_ref, v_ref, z_ref, lq1_ref, lk1_ref, lq2_ref, lk2_ref, g_ref, o_ref,
                 qs_ref, sa_ref, sb_ref, m_ref, acc_ref, *, tq):
    i = pl.program_id(2)
    q = q_ref[...]
    lane = lax.broadcasted_iota(jnp.int32, q.shape, 1)
    zero = jnp.zeros_like(q)
    qs_ref[0:tq, :] = jnp.where(lane < A_QK_DIM, q, zero)
    qs_ref[tq:2 * tq, :] = jnp.where(lane >= A_QK_DIM, q, zero)
    _flash_init(m_ref, acc_ref)

    def scores(j, s_ref):
        s_ref[...] = _dot_nt(qs_ref[...], k_ref[pl.ds(pl.multiple_of(j * tq, tq), tq), :])

    def update(j, s_ref, causal=False):
        s = s_ref[...]
        if causal:
            row = lax.broadcasted_iota(jnp.int32, s.shape, 0)
            col = lax.broadcasted_iota(jnp.int32, s.shape, 1)
            s = jnp.where(col <= (row & (tq - 1)), s, NEG)
        _flash_update(s, v_ref[pl.ds(pl.multiple_of(j * tq, tq), tq), :], m_ref, acc_ref)

    scores(0, sa_ref)

    def pair(jj, carry):
        j = 2 * jj
        scores(j + 1, sb_ref)
        update(j, sa_ref)
        scores(j + 2, sa_ref)
        update(j + 1, sb_ref)
        return carry

    lax.fori_loop(0, i // 2, pair, 0)

    @pl.when(i % 2 == 1)
    def _():
        scores(i, sb_ref)
        update(i - 1, sa_ref)
        update(i, sb_ref, causal=True)

    @pl.when(i % 2 == 0)
    def _():
        update(i, sa_ref, causal=True)

    o = _flash_result(acc_ref)
    lam = (jnp.exp(jnp.sum(lq1_ref[...] * lk1_ref[...], axis=-1, keepdims=True))
           - jnp.exp(jnp.sum(lq2_ref[...] * lk2_ref[...], axis=-1, keepdims=True)) + LAM_INIT)
    d = o[0:tq] - lam * o[tq:2 * tq]
    dn = d * lax.rsqrt(jnp.mean(d * d, axis=-1, keepdims=True) + EPS)
    dn = dn * g_ref[...] * (1.0 - LAM_INIT)
    o_ref[...] = (dn * _silu(z_ref[...].astype(F32))).astype(o_ref.dtype)


def _diff_call(p3, lq1, lk1, lq2, lk2, sub_g):
    b, s, _ = p3.shape
    tq = 512
    lam_spec = pl.BlockSpec((1, A_QK_DIM), lambda bi, h, i: (0, 0))
    return pl.pallas_call(
        functools.partial(_diff_kernel, tq=tq),
        grid=(b, A_HEADS, s // tq),
        in_specs=[pl.BlockSpec((None, tq, LANE), lambda bi, h, i: (bi, i, CB_AQ + h)),
                  pl.BlockSpec((None, s, LANE), lambda bi, h, i: (bi, 0, CB_AK + h)),
                  pl.BlockSpec((None, s, LANE), lambda bi, h, i: (bi, 0, CB_AV + h)),
                  pl.BlockSpec((None, tq, LANE), lambda bi, h, i: (bi, i, CB_AZ + h)),
                  lam_spec, lam_spec, lam_spec, lam_spec,
                  pl.BlockSpec((1, A_V_DIM), lambda bi, h, i: (0, 0))],
        out_specs=pl.BlockSpec((None, tq, LANE), lambda bi, h, i: (bi, i, h)),
        out_shape=jax.ShapeDtypeStruct((b, s, A_HEADS * A_V_DIM), BF16),
        scratch_shapes=[pltpu.VMEM((2 * tq, LANE), BF16),
                        pltpu.VMEM((2 * tq, tq), F32),
                        pltpu.VMEM((2 * tq, tq), F32),
                        pltpu.VMEM((2 * tq, LANE), F32),
                        pltpu.VMEM((2 * tq, 2 * LANE), F32)],
        compiler_params=pltpu.CompilerParams(
            dimension_semantics=("arbitrary", "arbitrary", "arbitrary"), vmem_limit_bytes=VMEM_LIMIT),
        name="diff_attn",
    )(p3, p3, p3, p3, lq1, lk1, lq2, lk2, sub_g)


def _compress_kernel(xk_ref, xv_ref, pek_ref, pev_ref, w1k_ref, w2k_ref, w1v_ref, w2v_ref,
                     c_ref, s1_ref, s2_ref, kc_ref, vc_ref):
    half_w = xk_ref.shape[1]
    n = xk_ref.shape[0]

    def mlp(x_ref, pe_ref, w1_ref, w2_ref):
        x = x_ref[...].astype(F32)
        top = _dot((x + pe_ref[0:1, :]).astype(BF16), w1_ref[0:half_w, :])
        bot = _dot((x + pe_ref[1:2, :]).astype(BF16), w1_ref[half_w:2 * half_w, :])
        hid = top + pltpu.roll(bot, n - 1, axis=0)
        return _dot(_silu(hid).astype(BF16), w2_ref[...])

    kc = mlp(xk_ref, pek_ref, w1k_ref, w2k_ref)
    half = B_HEAD_DIM // ROPE_FRACTION // 2
    up = pltpu.roll(kc, LANE - half, axis=1)
    dn = pltpu.roll(kc, half, axis=1)
    kc_ref[...] = (kc * c_ref[...] + up * s1_ref[...] + dn * s2_ref[...]).astype(kc_ref.dtype)
    vc_ref[...] = mlp(xv_ref, pev_ref, w1v_ref, w2v_ref).astype(vc_ref.dtype)


def _compress_call(xk, xv, pek, pev, w1k, w2k, w1v, w2v, tabs):
    b, g, n, hw = xk.shape
    x_spec = pl.BlockSpec((None, None, n, hw), lambda bi, gi: (bi, gi, 0, 0))
    o_spec = pl.BlockSpec((None, None, n, B_HEAD_DIM), lambda bi, gi: (bi, gi, 0, 0))

    def full(a):
        return pl.BlockSpec(a.shape, lambda bi, gi: (0,) * a.ndim)

    consts = (pek, pev, w1k, w2k, w1v, w2v, *tabs)
    return pl.pallas_call(
        _compress_kernel,
        grid=(b, g),
        in_specs=[x_spec, x_spec] + [full(a) for a in consts],
        out_specs=[o_spec, o_spec],
        out_shape=[jax.ShapeDtypeStruct((b, g, n, B_HEAD_DIM), BF16)] * 2,
        compiler_params=pltpu.CompilerParams(
            dimension_semantics=("arbitrary", "arbitrary"), vmem_limit_bytes=VMEM_LIMIT),
        name="compress",
    )(xk, xv, *consts)


def _select_blocks(imp_t, s0, n_slc, n_top):
    shape = imp_t.shape
    jb = lax.broadcasted_iota(jnp.int32, shape, 0)
    jbf = jb.astype(F32)
    qpos = s0 + lax.broadcasted_iota(jnp.int32, shape, 1)
    cur = qpos // SLC_BLOCK
    valid = (jb * SLC_BLOCK <= qpos) & (jb < n_slc)
    forced = (jb == 0) | (jb == cur) | (jb == cur - 1)
    imp = jnp.where(forced & valid, FORCE_SCORE, imp_t)
    imp = jnp.where(valid, imp, NEG)
    bias = jnp.full(shape, NEG, F32)
    for _ in range(n_top):
        mx = jnp.max(imp, axis=0, keepdims=True)
        idx = jnp.min(jnp.where(imp == mx, jbf, float(LANE)), axis=0, keepdims=True)
        hit = jbf == idx
        bias = jnp.where(hit & (mx > NEG * 0.5), 0.0, bias)
        imp = jnp.where(hit, -3.0e38, imp)
    return bias


def _nsa_kernel(q_ref, kc_ref, vc_ref, ks_ref, vs_ref, kw_ref, vw_ref, e_ref, ovt_ref, gate_ref, z_ref,
                o_ref, qa_ref, sa_ref, sb_ref, m_ref, acc_ref, *, tq, tk, n_slc, n_top):
    i = pl.program_id(2)
    s0 = i * tq
    hd = B_HEAD_DIM
    rows = B_HPG * tq
    qs = jnp.concatenate([q_ref[:, h * hd:(h + 1) * hd] for h in range(B_HPG)], axis=0)
    rid = lax.broadcasted_iota(jnp.int32, (rows, 1), 0)
    qpos_r = s0 + (rid & (tq - 1))

    n_cmp_pad = kc_ref.shape[0]
    cend = lax.broadcasted_iota(jnp.int32, (1, n_cmp_pad), 1) * CMP_STRIDE + (CMP_LEN - 1)
    cmask = cend <= qpos_r
    sc = jnp.where(cmask, _dot_nt(qs, kc_ref[...]), NEG)
    pe = jnp.where(cmask, jnp.exp2(sc - jnp.max(sc, axis=-1, keepdims=True)), 0.0)
    lsum = jnp.sum(pe, axis=-1, keepdims=True)
    pc = pe * jnp.where(lsum > 0.0, 1.0 / lsum, 0.0)
    o_cmp = _dot(pc.astype(BF16), vc_ref[...])
    psum = pc[0:tq]
    for h in range(1, B_HPG):
        psum = psum + pc[h * tq:(h + 1) * tq]
    imp_t = _dot_nt(ovt_ref[...], psum.astype(BF16))

    selb16 = _select_blocks(imp_t, s0, n_slc, n_top).T.astype(BF16)
    for h in range(B_HPG):
        qa_ref[h * tq:(h + 1) * tq, 0:hd] = q_ref[:, h * hd:(h + 1) * hd]
        qa_ref[h * tq:(h + 1) * tq, hd:2 * hd] = selb16
    _flash_init(m_ref, acc_ref)

    def slc_scores(j, s_ref):
        k0 = pl.multiple_of(j * tk, tk)
        ka = jnp.concatenate([ks_ref[pl.ds(k0, tk), :], e_ref[pl.ds(k0, tk), :]], axis=1)
        s_ref[...] = _dot_nt(qa_ref[...], ka)

    def slc_update(j, s_ref, causal=False):
        k0 = pl.multiple_of(j * tk, tk)
        s = s_ref[...]
        if causal:
            kpos = k0 + lax.broadcasted_iota(jnp.int32, (1, tk), 1)
            s = jnp.where(kpos <= qpos_r, s, NEG)
        _flash_update(s, vs_ref[pl.ds(k0, tk), :], m_ref, acc_ref)

    slc_scores(0, sa_ref)

    wlen = WINDOW + tq
    w0 = pl.multiple_of(jnp.maximum(s0 - WINDOW, 0), tq)
    wpos = w0 + lax.broadcasted_iota(jnp.int32, (1, wlen), 1)
    wmask = (wpos <= qpos_r) & (wpos > qpos_r - WINDOW)
    o_win = _masked_attend(_dot_nt(qs, kw_ref[pl.ds(w0, wlen), :]), wmask, vw_ref[pl.ds(w0, wlen), :])

    def pair(jj, carry):
        j = 2 * jj
        slc_scores(j + 1, sb_ref)
        slc_update(j, sa_ref)
        slc_scores(j + 2, sa_ref)
        slc_update(j + 1, sb_ref)
        return carry

    jd = s0 // tk
    lax.fori_loop(0, jd // 2, pair, 0)

    @pl.when(jd % 2 == 1)
    def _():
        slc_scores(jd, sb_ref)
        slc_update(jd - 1, sa_ref)
        slc_update(jd, sb_ref, causal=True)

    @pl.when(jd % 2 == 0)
    def _():
        slc_update(jd, sa_ref, causal=True)

    o_slc = _flash_result(acc_ref)

    gates = jax.nn.sigmoid(gate_ref[...].astype(F32))
    for h in range(B_HPG):
        r = slice(h * tq, (h + 1) * tq)
        o = (gates[:, 3 * h:3 * h + 1] * o_cmp[r] + gates[:, 3 * h + 1:3 * h + 2] * o_slc[r]
             + gates[:, 3 * h + 2:3 * h + 3] * o_win[r])
        z = z_ref[:, h * hd:(h + 1) * hd].astype(F32)
        o_ref[:, h * hd:(h + 1) * hd] = (o * _silu(z)).astype(o_ref.dtype)


def _nsa_call(p3, kc, vc, e_mat, ovt_mat):
    b, s, _ = p3.shape
    tq, tk = 256, 512
    n_slc = s // SLC_BLOCK
    n_top = min(SLC_TOPK, n_slc)
    gw = B_HPG * B_HEAD_DIM
    rows = B_HPG * tq
    n_cmp_pad = kc.shape[2]

    def slab(cb):
        return pl.BlockSpec((None, s, LANE), lambda bi, g, i: (bi, 0, cb + g))

    cmp_spec = pl.BlockSpec((None, None, n_cmp_pad, B_HEAD_DIM), lambda bi, g, i: (bi, g, 0, 0))
    return pl.pallas_call(
        functools.partial(_nsa_kernel, tq=tq, tk=tk, n_slc=n_slc, n_top=n_top),
        grid=(b, B_GROUPS, s // tq),
        in_specs=[pl.BlockSpec((None, tq, gw), lambda bi, g, i: (bi, i, CB_BQ * LANE // gw + g)),
                  cmp_spec, cmp_spec,
                  slab(CB_BKS), slab(CB_BVS), slab(CB_BKW), slab(CB_BVW),
                  pl.BlockSpec((s, LANE), lambda bi, g, i: (0, 0)),
                  pl.BlockSpec((LANE, n_cmp_pad), lambda bi, g, i: (0, 0)),
                  pl.BlockSpec((None, tq, LANE), lambda bi, g, i: (bi, i, CB_BGATE + g)),
                  pl.BlockSpec((None, tq, gw), lambda bi, g, i: (bi, i, CB_BZ * LANE // gw + g))],
        out_specs=pl.BlockSpec((None, tq, gw), lambda bi, g, i: (bi, i, g)),
        out_shape=jax.ShapeDtypeStruct((b, s, B_HEADS * B_HEAD_DIM), BF16),
        scratch_shapes=[pltpu.VMEM((rows, 2 * B_HEAD_DIM), BF16),
                        pltpu.VMEM((rows, tk), F32),
                        pltpu.VMEM((rows, tk), F32),
                        pltpu.VMEM((rows, LANE), F32),
                        pltpu.VMEM((rows, 2 * LANE), F32)],
        compiler_params=pltpu.CompilerParams(
            dimension_semantics=("arbitrary", "arbitrary", "arbitrary"), vmem_limit_bytes=VMEM_LIMIT),
        name="nsa",
    )(p3, kc, vc, p3, p3, p3, p3, e_mat, ovt_mat, p3, p3)


def _out_kernel(oa_ref, ob_ref, ga_ref, gb_ref, x_ref, gate_ref, wa_ref, wb_ref, wo_ref, fg_ref, o_ref):
    ya = _dot(oa_ref[...], wa_ref[...])
    yb = _dot(ob_ref[...], wb_ref[...])
    mix = (jax.nn.sigmoid(ga_ref[...].astype(F32)) * ya
           + jax.nn.sigmoid(gb_ref[...].astype(F32)) * yb)
    y = _dot(mix.astype(BF16), wo_ref[...])
    xo = x_ref[...] + gate_ref[...] * y
    o_ref[...] = xo * lax.rsqrt(jnp.mean(xo * xo, axis=-1, keepdims=True) + EPS) * fg_ref[...]


def _out_call(oa, ob, p2, x2, gate, wa, wb, wo, fg, seq):
    t, d = x2.shape
    tm = 256
    per_seq = seq // tm
    aw = oa.shape[1]
    bw = ob.shape[1]

    def resident(a):
        return pl.BlockSpec(a.shape, lambda i: (0, 0), pipeline_mode=pl.Buffered(1))

    return pl.pallas_call(
        _out_kernel,
        grid=(t // tm,),
        in_specs=[pl.BlockSpec((tm, aw), lambda i: (i, 0)),
                  pl.BlockSpec((tm, bw), lambda i: (i, 0)),
                  pl.BlockSpec((tm, d), lambda i: (i, CB_GA * LANE // d)),
                  pl.BlockSpec((tm, d), lambda i: (i, CB_GB * LANE // d)),
                  pl.BlockSpec((tm, d), lambda i: (i, 0)),
                  pl.BlockSpec((None, 1, d), lambda i: (i // per_seq, 0, 0)),
                  resident(wa), resident(wb), resident(wo),
                  pl.BlockSpec((1, d), lambda i: (0, 0))],
        out_specs=pl.BlockSpec((tm, d), lambda i: (i, 0)),
        out_shape=jax.ShapeDtypeStruct((t, d), F32),
        compiler_params=pltpu.CompilerParams(
            dimension_semantics=("arbitrary",), vmem_limit_bytes=VMEM_LIMIT),
        name="out_proj",
    )(oa, ob, p2, p2, x2, gate, wa, wb, wo, fg)


def _rope_tables(pos, head_dim):
    rd = head_dim // ROPE_FRACTION
    half = rd // 2
    inv = 1.0 / (ROPE_THETA ** (jnp.arange(half, dtype=F32) * (2.0 / rd)))
    ang = pos.astype(F32)[:, None] * inv[None, :]
    cos, sin = jnp.cos(ang), jnp.sin(ang)
    n = pos.shape[0]
    pad = jnp.zeros((n, head_dim - rd), F32)
    zero = jnp.zeros((n, half), F32)
    c = jnp.concatenate([cos, cos, pad + 1.0], axis=1)
    s1 = jnp.concatenate([-sin, zero, pad], axis=1)
    s2 = jnp.concatenate([zero, sin, pad], axis=1)
    reps = LANE // head_dim
    return tuple(jnp.tile(a, (1, reps)) for a in (c, s1, s2))


def _rearranged_w_in(w):
    off = {}
    o = 0
    widths = (("aq", 1024), ("ak", 1024), ("av", 1024), ("az", 1024), ("bq", 1024), ("bkc", 256),
              ("bvc", 256), ("bks", 256), ("bvs", 256), ("bkw", 256), ("bvw", 256), ("bz", 1024),
              ("bgate", 24), ("mgate", 4096))
    for name, wd in widths:
        off[name] = (o, wd)
        o += wd

    def col(name, scale=None):
        a, wd = off[name]
        c = w[:, a:a + wd]
        return c if scale is None else c * scale

    d = w.shape[0]
    gpg = B_HPG * 3
    a_g, _ = off["bgate"]
    zpad = jnp.zeros((d, LANE - gpg), w.dtype)
    gate_cols = [w[:, a_g:a_g + gpg], zpad, w[:, a_g + gpg:a_g + 2 * gpg], zpad,
                 jnp.zeros((d, 2 * LANE), w.dtype)]
    parts = [col("aq", LOG2E * A_QK_DIM ** -0.5), col("ak"), col("bq", LOG2E * B_HEAD_DIM ** -0.5),
             col("bks"), col("bkw"),
             col("av"), col("az"), col("bkc"), col("bvc"), col("bvs"), col("bvw"), col("bz"),
             *gate_cols, col("mgate")]
    return jnp.concatenate(parts, axis=1).astype(BF16)


def kernel(x, c, w_ada, b_ada, norm_g, w_in, lambda_q1, lambda_k1, lambda_q2, lambda_k2, diff_norm_g,
           cmp_pe_k, cmp_pe_v, cmp_w1_k, cmp_w2_k, cmp_w1_v, cmp_w2_v, w_branch, w_out, final_norm_g):
    b, s, d = x.shape
    assert d == D_MODEL and s % 1024 == 0 and s // SLC_BLOCK <= LANE
    t = b * s
    x2 = x.reshape(t, d)

    c_pad = jnp.pad(c, ((0, 8 - b % 8 if b % 8 else 0), (0, 0)))
    mod = _ada_call(c_pad, w_ada[0], b_ada[0][None, :])[:b]
    shift = mod[:, None, 0:d]
    scale = mod[:, None, d:2 * d]
    gate = mod[:, None, 2 * d:3 * d]

    pos = jnp.arange(s)
    p2 = _inproj_call(x2, norm_g[0][None, :], scale, shift, _rearranged_w_in(w_in[0]),
                      _rope_tables(pos, A_QK_DIM), _rope_tables(pos, B_HEAD_DIM), s)
    p3 = p2.reshape(b, s, P_WIDTH)

    oa = _diff_call(p3, lambda_q1, lambda_k1, lambda_q2, lambda_k2, diff_norm_g)

    n_str = s // CMP_STRIDE
    hw = CMP_STRIDE * B_HEAD_DIM

    def strides(cb):
        a = p3[:, :, cb * LANE:(cb + B_GROUPS) * LANE].reshape(b, n_str, CMP_STRIDE, B_GROUPS, B_HEAD_DIM)
        return a.transpose(0, 3, 1, 2, 4).reshape(b, B_GROUPS, n_str, hw)

    cmp_end = jnp.arange(n_str) * CMP_STRIDE + (CMP_LEN - 1)
    kc, vc = _compress_call(
        strides(CB_BKC), strides(CB_BVC),
        cmp_pe_k[0].reshape(2, hw), cmp_pe_v[0].reshape(2, hw),
        cmp_w1_k[0].astype(BF16), cmp_w2_k[0].astype(BF16),
        cmp_w1_v[0].astype(BF16), cmp_w2_v[0].astype(BF16),
        _rope_tables(cmp_end, B_HEAD_DIM))
    e_mat = (jnp.arange(s)[:, None] // SLC_BLOCK == jnp.arange(LANE)[None, :]).astype(BF16)
    cmp_start = jnp.arange(n_str) * CMP_STRIDE
    slc_start = jnp.arange(LANE) * SLC_BLOCK
    ovt_mat = ((cmp_start[None, :] < slc_start[:, None] + SLC_BLOCK)
               & (cmp_start[None, :] + CMP_LEN > slc_start[:, None])).astype(BF16)
    ob = _nsa_call(p3, kc, vc, e_mat, ovt_mat)

    wbr = w_branch[0].astype(BF16)
    a_w = A_HEADS * A_V_DIM
    out = _out_call(oa.reshape(t, a_w), ob.reshape(t, B_HEADS * B_HEAD_DIM), p2, x2, gate,
                    wbr[:a_w], wbr[a_w:], w_out[0].astype(BF16), final_norm_g[None, :], s)
    return out.reshape(b, s, d)
```

```python
import functools
import math

import jax
import jax.numpy as jnp
from jax import lax
from jax.experimental import pallas as pl
from jax.experimental.pallas import tpu as pltpu

F32 = jnp.float32
BF16 = jnp.bfloat16

D_MODEL = 2048
A_HEADS = 8
A_QK_DIM = 64
A_V_DIM = 128
B_HEADS = 8
B_GROUPS = 2
B_HPG = B_HEADS // B_GROUPS
B_HEAD_DIM = 128
CMP_LEN = 32
CMP_STRIDE = 16
CMP_HIDDEN = 256
SLC_BLOCK = 64
SLC_TOPK = 16
N_FORCED = 3
WINDOW = 512
FORCE_SCORE = 1.0e4
ROPE_THETA = 500000.0
ROPE_FRACTION = 4
EPS = 1e-6
NEG = -1e30
TAKEN = -3.0e38
LAM_INIT = 0.8 - 0.6 * math.exp(-0.3 * 0)
LOG2E = math.log2(math.e)

LANE = 128
SUBLANE = 8
VMEM_LIMIT = 56 * 1024 * 1024

CB_AQ, CB_AK, CB_BQ, CB_BKS, CB_BKW = 0, 8, 16, 24, 26
CB_AV, CB_AZ, CB_BKC, CB_BVC, CB_BVS, CB_BVW, CB_BZ = 28, 36, 44, 46, 48, 50, 52
CB_BGATE, CB_GA, CB_GB = 60, 64, 80
P_WIDTH = 96 * LANE
PROJ_TN = 512


def _dot(a, b):
    return jnp.dot(a, b, preferred_element_type=F32)


def _dot_nt(a, b):
    return lax.dot_general(a, b, (((1,), (1,)), ((), ())), preferred_element_type=F32)


def _silu(v):
    return v * jax.nn.sigmoid(v)


def _rope(x, tabs, half):
    c, s1, s2 = tabs
    return x * c + pltpu.roll(x, LANE - half, axis=1) * s1 + pltpu.roll(x, half, axis=1) * s2


def _ada_kernel(c_ref, w_ref, b_ref, o_ref):
    o_ref[...] = _dot(_silu(c_ref[...]), w_ref[...]) + b_ref[...]


def _ada_call(c_pad, w, b):
    rows, d = c_pad.shape
    n = w.shape[1]
    tn = 768
    return pl.pallas_call(
        _ada_kernel,
        grid=(n // tn,),
        in_specs=[pl.BlockSpec((rows, d), lambda j: (0, 0)),
                  pl.BlockSpec((d, tn), lambda j: (0, j)),
                  pl.BlockSpec((1, tn), lambda j: (0, j))],
        out_specs=pl.BlockSpec((rows, tn), lambda j: (0, j)),
        out_shape=jax.ShapeDtypeStruct((rows, n), F32),
        compiler_params=pltpu.CompilerParams(vmem_limit_bytes=VMEM_LIMIT),
        name="ada",
    )(c_pad, w, b)


def _inproj_kernel(x_ref, g_ref, sc_ref, sh_ref, w_ref, o_ref, h_ref):
    @pl.when(pl.program_id(1) == 0)
    def _():
        x = x_ref[...]
        ms = jnp.mean(x * x, axis=-1, keepdims=True)
        xn = x * lax.rsqrt(ms + EPS)
        h_ref[...] = (xn * (g_ref[...] * (1.0 + sc_ref[...])) + sh_ref[...]).astype(h_ref.dtype)

    o_ref[...] = _dot(h_ref[...], w_ref[...]).astype(o_ref.dtype)


def _inproj_call(x2, g, scale, shift, w, seq):
    t, d = x2.shape
    tm = min(1024, seq)
    per_seq = seq // tm
    mod = pl.BlockSpec((None, 1, d), lambda i, j: (i // per_seq, 0, 0))
    return pl.pallas_call(
        _inproj_kernel,
        grid=(t // tm, P_WIDTH // PROJ_TN),
        in_specs=[pl.BlockSpec((tm, d), lambda i, j: (i, 0)),
                  pl.BlockSpec((1, d), lambda i, j: (0, 0)),
                  mod, mod,
                  pl.BlockSpec((d, PROJ_TN), lambda i, j: (0, j))],
        out_specs=pl.BlockSpec((tm, PROJ_TN), lambda i, j: (i, j)),
        out_shape=jax.ShapeDtypeStruct((t, P_WIDTH), BF16),
        scratch_shapes=[pltpu.VMEM((tm, d), BF16)],
        compiler_params=pltpu.CompilerParams(
            dimension_semantics=("arbitrary", "arbitrary"), vmem_limit_bytes=VMEM_LIMIT),
        name="inproj",
    )(x2, g, scale, shift, w)


def _flash_init(m_ref, acc_ref):
    m_ref[...] = jnp.full(m_ref.shape, NEG, F32)
    acc_ref[...] = jnp.zeros(acc_ref.shape, F32)


def _with_ones(v):
    return jnp.concatenate([v, jnp.ones(v.shape, v.dtype)], axis=1)


def _flash_update(s, v, m_ref, acc_ref):
    m_old = m_ref[...]
    m_new = jnp.maximum(m_old, jnp.max(s, axis=-1, keepdims=True))
    alpha = jnp.exp2(m_old - m_new)
    p = jnp.exp2(s - jnp.tile(m_new, (1, s.shape[1] // LANE)))
    acc_ref[...] = jnp.tile(alpha, (1, 2)) * acc_ref[...] + _dot(p.astype(v.dtype), _with_ones(v))
    m_ref[...] = m_new


def _flash_result(acc_ref):
    acc = acc_ref[...]
    return acc[:, 0:LANE] / acc[:, LANE:2 * LANE]


def _masked_attend(s, mask, v):
    s = jnp.where(mask, s, NEG)
    p = jnp.exp2(s - jnp.max(s, axis=-1, keepdims=True))
    pv = _dot(p.astype(v.dtype), _with_ones(v))
    return pv[:, 0:LANE] / pv[:, LANE:2 * LANE]


def _diff_kernel(q_ref, k_ref, v_ref, z_ref, c_ref, s1_ref, s2_ref, lq1_ref, lk1_ref, lq2_ref, lk2_ref,
                 g_ref, o_ref, qs_ref, kr_ref, sa_ref, sb_ref, m_ref, acc_ref, *, tq):
    i = pl.program_id(2)
    tabs = (c_ref[...], s1_ref[...], s2_ref[...])
    half = A_QK_DIM // ROPE_FRACTION // 2
    q = _rope(q_ref[...].astype(F32), tabs, half)
    lane = lax.broadcasted_iota(jnp.int32, q.shape, 1)
    qs_ref[0:tq, :] = jnp.where(lane < A_QK_DIM, q, 0.0).astype(qs_ref.dtype)
    qs_ref[tq:2 * tq, :] = jnp.where(lane >= A_QK_DIM, q, 0.0).astype(qs_ref.dtype)
    kr_ref[pl.ds(pl.multiple_of(i * tq, tq), tq), :] = _rope(k_ref[...].astype(F32), tabs, half).astype(kr_ref.dtype)
    _flash_init(m_ref, acc_ref)

    def scores(j, s_ref):
        s_ref[...] = _dot_nt(qs_ref[...], kr_ref[pl.ds(pl.multiple_of(j * tq, tq), tq), :])

    def update(j, s_ref, causal=False):
        s = s_ref[...]
        if causal:
            row = lax.broadcasted_iota(jnp.int32, s.shape, 0)
            col = lax.broadcasted_iota(jnp.int32, s.shape, 1)
            s = jnp.where(col <= (row & (tq - 1)), s, NEG)
        _flash_update(s, v_ref[pl.ds(pl.multiple_of(j * tq, tq), tq), :], m_ref, acc_ref)

    scores(0, sa_ref)

    def pair(jj, carry):
        j = 2 * jj
        scores(j + 1, sb_ref)
        update(j, sa_ref)
        scores(j + 2, sa_ref)
        update(j + 1, sb_ref)
        return carry

    lax.fori_loop(0, i // 2, pair, 0)

    @pl.when(i % 2 == 1)
    def _():
        scores(i, sb_ref)
        update(i - 1, sa_ref)
        update(i, sb_ref, causal=True)

    @pl.when(i % 2 == 0)
    def _():
        update(i, sa_ref, causal=True)

    o = _flash_result(acc_ref)
    lam = (jnp.exp(jnp.sum(lq1_ref[...] * lk1_ref[...], axis=-1, keepdims=True))
           - jnp.exp(jnp.sum(lq2_ref[...] * lk2_ref[...], axis=-1, keepdims=True)) + LAM_INIT)
    d = o[0:tq] - lam * o[tq:2 * tq]
    dn = d * lax.rsqrt(jnp.mean(d * d, axis=-1, keepdims=True) + EPS)
    dn = dn * g_ref[...] * (1.0 - LAM_INIT)
    o_ref[...] = (dn * _silu(z_ref[...].astype(F32))).astype(o_ref.dtype)


def _diff_call(p3, tabs, lq1, lk1, lq2, lk2, sub_g):
    b, s, _ = p3.shape
    tq = 512
    lam_spec = pl.BlockSpec((1, A_QK_DIM), lambda bi, h, i: (0, 0))
    tab_spec = pl.BlockSpec((tq, LANE), lambda bi, h, i: (i, 0))
    return pl.pallas_call(
        functools.partial(_diff_kernel, tq=tq),
        grid=(b, A_HEADS, s // tq),
        in_specs=[pl.BlockSpec((None, tq, LANE), lambda bi, h, i: (bi, i, CB_AQ + h)),
                  pl.BlockSpec((None, tq, LANE), lambda bi, h, i: (bi, i, CB_AK + h)),
                  pl.BlockSpec((None, s, LANE), lambda bi, h, i: (bi, 0, CB_AV + h)),
                  pl.BlockSpec((None, tq, LANE), lambda bi, h, i: (bi, i, CB_AZ + h)),
                  tab_spec, tab_spec, tab_spec,
                  lam_spec, lam_spec, lam_spec, lam_spec,
                  pl.BlockSpec((1, A_V_DIM), lambda bi, h, i: (0, 0))],
        out_specs=pl.BlockSpec((None, tq, LANE), lambda bi, h, i: (bi, i, h)),
        out_shape=jax.ShapeDtypeStruct((b, s, A_HEADS * A_V_DIM), BF16),
        scratch_shapes=[pltpu.VMEM((2 * tq, LANE), BF16),
                        pltpu.VMEM((s, LANE), BF16),
                        pltpu.VMEM((2 * tq, tq), F32),
                        pltpu.VMEM((2 * tq, tq), F32),
                        pltpu.VMEM((2 * tq, LANE), F32),
                        pltpu.VMEM((2 * tq, 2 * LANE), F32)],
        compiler_params=pltpu.CompilerParams(
            dimension_semantics=("arbitrary", "arbitrary", "arbitrary"), vmem_limit_bytes=VMEM_LIMIT),
        name="diff_attn",
    )(p3, p3, p3, p3, *tabs, lq1, lk1, lq2, lk2, sub_g)


def _compress_kernel(xk_ref, xv_ref, pek_ref, pev_ref, w1k_ref, w2k_ref, w1v_ref, w2v_ref,
                     c_ref, s1_ref, s2_ref, kc_ref, vc_ref):
    half_w = xk_ref.shape[1]
    n = xk_ref.shape[0]

    def mlp(x_ref, pe_ref, w1_ref, w2_ref):
        x = x_ref[...].astype(F32)
        top = _dot((x + pe_ref[0:1, :]).astype(BF16), w1_ref[0:half_w, :])
        bot = _dot((x + pe_ref[1:2, :]).astype(BF16), w1_ref[half_w:2 * half_w, :])
        hid = top + pltpu.roll(bot, n - 1, axis=0)
        return _dot(_silu(hid).astype(BF16), w2_ref[...])

    kc = mlp(xk_ref, pek_ref, w1k_ref, w2k_ref)
    tabs = (c_ref[...], s1_ref[...], s2_ref[...])
    kc_ref[...] = _rope(kc, tabs, B_HEAD_DIM // ROPE_FRACTION // 2).astype(kc_ref.dtype)
    vc_ref[...] = mlp(xv_ref, pev_ref, w1v_ref, w2v_ref).astype(vc_ref.dtype)


def _compress_call(xk, xv, pek, pev, w1k, w2k, w1v, w2v, tabs):
    b, g, n, hw = xk.shape
    x_spec = pl.BlockSpec((None, None, n, hw), lambda bi, gi: (bi, gi, 0, 0))
    o_spec = pl.BlockSpec((None, None, n, B_HEAD_DIM), lambda bi, gi: (bi, gi, 0, 0))

    def full(a):
        return pl.BlockSpec(a.shape, lambda bi, gi: (0,) * a.ndim)

    consts = (pek, pev, w1k, w2k, w1v, w2v, *tabs)
    return pl.pallas_call(
        _compress_kernel,
        grid=(b, g),
        in_specs=[x_spec, x_spec] + [full(a) for a in consts],
        out_specs=[o_spec, o_spec],
        out_shape=[jax.ShapeDtypeStruct((b, g, n, B_HEAD_DIM), BF16)] * 2,
        compiler_params=pltpu.CompilerParams(
            dimension_semantics=("arbitrary", "arbitrary"), vmem_limit_bytes=VMEM_LIMIT),
        name="compress",
    )(xk, xv, *consts)


def _select_blocks(imp_t, s0, n_slc, n_top):
    shape = imp_t.shape
    jb = lax.broadcasted_iota(jnp.int32, shape, 0)
    jbf = jb.astype(F32)
    qpos = s0 + lax.broadcasted_iota(jnp.int32, shape, 1)
    cur = qpos // SLC_BLOCK
    valid = (jb * SLC_BLOCK <= qpos) & (jb < n_slc)
    forced = (jb == 0) | (jb == cur) | (jb == cur - 1)
    imp = jnp.where(valid, jnp.where(forced, TAKEN, imp_t), NEG)
    for _ in range(n_top - N_FORCED):
        mx = jnp.max(imp, axis=0, keepdims=True)
        idx = jnp.min(jnp.where(imp == mx, jbf, float(LANE)), axis=0, keepdims=True)
        imp = jnp.where(jbf == idx, TAKEN, imp)
    return jnp.where((imp == TAKEN) & valid, 0.0, NEG)


def _nsa_kernel(q_ref, kc_ref, vc_ref, ks_ref, vs_ref, kw_ref, vw_ref, e_ref, ovt_ref, gate_ref, z_ref,
                c_ref, s1_ref, s2_ref, o_ref, qa_ref, ksr_ref, kwr_ref, sa_ref, sb_ref, m_ref, acc_ref,
                *, tq, tk, n_slc, n_top):
    i = pl.program_id(2)
    s0 = i * tq
    hd = B_HEAD_DIM
    rows = B_HPG * tq
    half = hd // ROPE_FRACTION // 2

    sub = pl.multiple_of((i % (tk // tq)) * tq, tq)
    qtabs = tuple(r[pl.ds(sub, tq), :] for r in (c_ref, s1_ref, s2_ref))
    qh = [_rope(q_ref[:, h * hd:(h + 1) * hd].astype(F32), qtabs, half).astype(BF16) for h in range(B_HPG)]
    qs = jnp.concatenate(qh, axis=0)

    @pl.when(i % (tk // tq) == 0)
    def _():
        ktabs = (c_ref[...], s1_ref[...], s2_ref[...])
        k0 = pl.multiple_of(s0, tk)
        ksr_ref[pl.ds(k0, tk), :] = _rope(ks_ref[...].astype(F32), ktabs, half).astype(BF16)
        kwr_ref[pl.ds(k0, tk), :] = _rope(kw_ref[...].astype(F32), ktabs, half).astype(BF16)

    rid = lax.broadcasted_iota(jnp.int32, (rows, 1), 0)
    qpos_r = s0 + (rid & (tq - 1))

    n_cmp_pad = kc_ref.shape[0]
    cend = lax.broadcasted_iota(jnp.int32, (1, n_cmp_pad), 1) * CMP_STRIDE + (CMP_LEN - 1)
    sc = jnp.where(cend <= qpos_r, _dot_nt(qs, kc_ref[...]), NEG)
    pc = jnp.exp2(sc - jnp.max(sc, axis=-1, keepdims=True)).astype(BF16)
    pv = _dot(pc, _with_ones(vc_ref[...]))
    o_cmp = pv[:, 0:LANE] * jnp.where(qpos_r >= CMP_LEN - 1, 1.0 / pv[:, LANE:2 * LANE], 0.0)
    qpos_t = s0 + lax.broadcasted_iota(jnp.int32, (1, tq), 1)
    imp_t = jnp.zeros((LANE, tq), F32)
    for h in range(B_HPG):
        r = _dot_nt(ovt_ref[...], pc[h * tq:(h + 1) * tq])
        imp_t = imp_t + r[0:LANE] * jnp.where(qpos_t >= CMP_LEN - 1, 1.0 / r[LANE:LANE + 1], 0.0)

    wlen = WINDOW + tq
    w0 = pl.multiple_of(jnp.maximum(s0 - WINDOW, 0), tq)
    s_win = _dot_nt(qs, kwr_ref[pl.ds(w0, wlen), :])

    selb16 =_select_blocks(imp_t, s0, n_slc, n_top).T.astype(BF16)
    for h in range(B_HPG):
        qa_ref[h * tq:(h + 1) * tq, 0:hd] = qh[h]
        qa_ref[h * tq:(h + 1) * tq, hd:2 * hd] = selb16
    _flash_init(m_ref, acc_ref)

    def slc_scores(j, s_ref):
        k0 = pl.multiple_of(j * tk, tk)
        ka = jnp.concatenate([ksr_ref[pl.ds(k0, tk), :], e_ref[pl.ds(k0, tk), :]], axis=1)
        s_ref[...] = _dot_nt(qa_ref[...], ka)

    def slc_update(j, s_ref, causal=False):
        k0 = pl.multiple_of(j * tk, tk)
        s = s_ref[...]
        if causal:
            kpos = k0 + lax.broadcasted_iota(jnp.int32, (1, tk), 1)
            s = jnp.where(kpos <= qpos_r, s, NEG)
        _flash_update(s, vs_ref[pl.ds(k0, tk), :], m_ref, acc_ref)

    slc_scores(0, sa_ref)

    wpos = w0 + lax.broadcasted_iota(jnp.int32, (1, wlen), 1)
    wmask = (wpos <= qpos_r) & (wpos > qpos_r - WINDOW)
    o_win = _masked_attend(s_win, wmask, vw_ref[pl.ds(w0, wlen), :])

    def pair(jj, carry):
        j = 2 * jj
        slc_scores(j + 1, sb_ref)
        slc_update(j, sa_ref)
        slc_scores(j + 2, sa_ref)
        slc_update(j + 1, sb_ref)
        return carry

    jd = s0 // tk
    lax.fori_loop(0, jd // 2, pair, 0)

    @pl.when(jd % 2 == 1)
    def _():
        slc_scores(jd, sb_ref)
        slc_update(jd - 1, sa_ref)
        slc_update(jd, sb_ref, causal=True)

    @pl.when(jd % 2 == 0)
    def _():
        slc_update(jd, sa_ref, causal=True)

    o_slc = _flash_result(acc_ref)

    gates = jax.nn.sigmoid(gate_ref[...].astype(F32))
    for h in range(B_HPG):
        r = slice(h * tq, (h + 1) * tq)
        o = (gates[:, 3 * h:3 * h + 1] * o_cmp[r] + gates[:, 3 * h + 1:3 * h + 2] * o_slc[r]
             + gates[:, 3 * h + 2:3 * h + 3] * o_win[r])
        z = z_ref[:, h * hd:(h + 1) * hd].astype(F32)
        o_ref[:, h * hd:(h + 1) * hd] = (o * _silu(z)).astype(o_ref.dtype)


def _nsa_call(p3, kc, vc, e_mat, ovt_mat, tabs):
    b, s, _ = p3.shape
    tq, tk = 256, 512
    n_slc = s // SLC_BLOCK
    n_top = min(SLC_TOPK, n_slc)
    gw = B_HPG * B_HEAD_DIM
    rows = B_HPG * tq
    n_cmp_pad = kc.shape[2]
    per_k = tk // tq

    def slab(cb):
        return pl.BlockSpec((None, s, LANE), lambda bi, g, i: (bi, 0, cb + g))

    def ktile(cb):
        return pl.BlockSpec((None, tk, LANE), lambda bi, g, i: (bi, i // per_k, cb + g))

    cmp_spec = pl.BlockSpec((None, None, n_cmp_pad, B_HEAD_DIM), lambda bi, g, i: (bi, g, 0, 0))
    tab_spec = pl.BlockSpec((tk, LANE), lambda bi, g, i: (i // per_k, 0))
    return pl.pallas_call(
        functools.partial(_nsa_kernel, tq=tq, tk=tk, n_slc=n_slc, n_top=n_top),
        grid=(b, B_GROUPS, s // tq),
        in_specs=[pl.BlockSpec((None, tq, gw), lambda bi, g, i: (bi, i, CB_BQ * LANE // gw + g)),
                  cmp_spec, cmp_spec,
                  ktile(CB_BKS), slab(CB_BVS), ktile(CB_BKW), slab(CB_BVW),
                  pl.BlockSpec((s, LANE), lambda bi, g, i: (0, 0)),
                  pl.BlockSpec((LANE + SUBLANE, n_cmp_pad), lambda bi, g, i: (0, 0)),
                  pl.BlockSpec((None, tq, LANE), lambda bi, g, i: (bi, i, CB_BGATE + g)),
                  pl.BlockSpec((None, tq, gw), lambda bi, g, i: (bi, i, CB_BZ * LANE // gw + g)),
                  tab_spec, tab_spec, tab_spec],
        out_specs=pl.BlockSpec((None, tq, gw), lambda bi, g, i: (bi, i, g)),
        out_shape=jax.ShapeDtypeStruct((b, s, B_HEADS * B_HEAD_DIM), BF16),
        scratch_shapes=[pltpu.VMEM((rows, 2 * B_HEAD_DIM), BF16),
                        pltpu.VMEM((s, LANE), BF16),
                        pltpu.VMEM((s, LANE), BF16),
                        pltpu.VMEM((rows, tk), F32),
                        pltpu.VMEM((rows, tk), F32),
                        pltpu.VMEM((rows, LANE), F32),
                        pltpu.VMEM((rows, 2 * LANE), F32)],
        compiler_params=pltpu.CompilerParams(
            dimension_semantics=("arbitrary", "arbitrary", "arbitrary"), vmem_limit_bytes=VMEM_LIMIT),
        name="nsa",
    )(p3, kc, vc, p3, p3, p3, p3, e_mat, ovt_mat, p3, p3, *tabs)


def _out_kernel(oa_ref, ob_ref, ga_ref, gb_ref, x_ref, gate_ref, wa_ref, wb_ref, wo_ref, fg_ref, o_ref):
    ya = _dot(oa_ref[...], wa_ref[...])
    yb = _dot(ob_ref[...], wb_ref[...])
    mix = (jax.nn.sigmoid(ga_ref[...].astype(F32)) * ya
           + jax.nn.sigmoid(gb_ref[...].astype(F32)) * yb)
    y = _dot(mix.astype(BF16), wo_ref[...])
    xo = x_ref[...] + gate_ref[...] * y
    o_ref[...] = xo * lax.rsqrt(jnp.mean(xo * xo, axis=-1, keepdims=True) + EPS) * fg_ref[...]


def _out_call(oa, ob, p2, x2, gate, wa, wb, wo, fg, seq):
    t, d = x2.shape
    tm = 256
    per_seq = seq // tm
    aw = oa.shape[1]
    bw = ob.shape[1]

    def resident(a):
        return pl.BlockSpec(a.shape, lambda i: (0, 0), pipeline_mode=pl.Buffered(1))

    return pl.pallas_call(
        _out_kernel,
        grid=(t // tm,),
        in_specs=[pl.BlockSpec((tm, aw), lambda i: (i, 0)),
                  pl.BlockSpec((tm, bw), lambda i: (i, 0)),
                  pl.BlockSpec((tm, d), lambda i: (i, CB_GA * LANE // d)),
                  pl.BlockSpec((tm, d), lambda i: (i, CB_GB * LANE // d)),
                  pl.BlockSpec((tm, d), lambda i: (i, 0)),
                  pl.BlockSpec((None, 1, d), lambda i: (i // per_seq, 0, 0)),
                  resident(wa), resident(wb), resident(wo),
                  pl.BlockSpec((1, d), lambda i: (0, 0))],
        out_specs=pl.BlockSpec((tm, d), lambda i: (i, 0)),
        out_shape=jax.ShapeDtypeStruct((t, d), F32),
        compiler_params=pltpu.CompilerParams(
            dimension_semantics=("arbitrary",), vmem_limit_bytes=VMEM_LIMIT),
        name="out_proj",
    )(oa, ob, p2, p2, x2, gate, wa, wb, wo, fg)


def _rope_tables(pos, head_dim):
    rd = head_dim // ROPE_FRACTION
    half = rd // 2
    inv = 1.0 / (ROPE_THETA ** (jnp.arange(half, dtype=F32) * (2.0 / rd)))
    ang = pos.astype(F32)[:, None] * inv[None, :]
    cos, sin = jnp.cos(ang), jnp.sin(ang)
    n = pos.shape[0]
    pad = jnp.zeros((n, head_dim - rd), F32)
    zero = jnp.zeros((n, half), F32)
    c = jnp.concatenate([cos, cos, pad + 1.0], axis=1)
    s1 = jnp.concatenate([-sin, zero, pad], axis=1)
    s2 = jnp.concatenate([zero, sin, pad], axis=1)
    reps = LANE // head_dim
    return tuple(jnp.tile(a, (1, reps)) for a in (c, s1, s2))


def _rearranged_w_in(w):
    off = {}
    o = 0
    widths = (("aq", 1024), ("ak", 1024), ("av", 1024), ("az", 1024), ("bq", 1024), ("bkc", 256),
              ("bvc", 256), ("bks", 256), ("bvs", 256), ("bkw", 256), ("bvw", 256), ("bz", 1024),
              ("bgate", 24), ("mgate", 4096))
    for name, wd in widths:
        off[name] = (o, wd)
        o += wd

    def col(name, scale=None):
        a, wd = off[name]
        c = w[:, a:a + wd]
        return c if scale is None else c * scale

    d = w.shape[0]
    gpg = B_HPG * 3
    a_g, _ = off["bgate"]
    zpad = jnp.zeros((d, LANE - gpg), w.dtype)
    gate_cols = [w[:, a_g:a_g + gpg], zpad, w[:, a_g + gpg:a_g + 2 * gpg], zpad,
                 jnp.zeros((d, 2 * LANE), w.dtype)]
    parts = [col("aq", LOG2E * A_QK_DIM ** -0.5), col("ak"), col("bq", LOG2E * B_HEAD_DIM ** -0.5),
             col("bks"), col("bkw"),
             col("av"), col("az"), col("bkc"), col("bvc"), col("bvs"), col("bvw"), col("bz"),
             *gate_cols, col("mgate")]
    return jnp.concatenate(parts, axis=1).astype(BF16)


def kernel(x, c, w_ada, b_ada, norm_g, w_in, lambda_q1, lambda_k1, lambda_q2, lambda_k2, diff_norm_g,
           cmp_pe_k, cmp_pe_v, cmp_w1_k, cmp_w2_k, cmp_w1_v, cmp_w2_v, w_branch, w_out, final_norm_g):
    b, s, d = x.shape
    assert d == D_MODEL and s % 1024 == 0 and s // SLC_BLOCK <= LANE
    t = b * s
    x2 = x.reshape(t, d)

    c_pad = jnp.pad(c, ((0, 8 - b % 8 if b % 8 else 0), (0, 0)))
    mod = _ada_call(c_pad, w_ada[0], b_ada[0][None, :])[:b]
    shift = mod[:, None, 0:d]
    scale = mod[:, None, d:2 * d]
    gate = mod[:, None, 2 * d:3 * d]

    p2 = _inproj_call(x2, norm_g[0][None, :], scale, shift, _rearranged_w_in(w_in[0]), s)
    p3 = p2.reshape(b, s, P_WIDTH)

    pos = jnp.arange(s)
    oa = _diff_call(p3, _rope_tables(pos, A_QK_DIM), lambda_q1, lambda_k1, lambda_q2, lambda_k2, diff_norm_g)

    n_str = s // CMP_STRIDE
    hw = CMP_STRIDE * B_HEAD_DIM

    def strides(cb):
        a = p3[:, :, cb * LANE:(cb + B_GROUPS) * LANE].reshape(b, n_str, CMP_STRIDE, B_GROUPS, B_HEAD_DIM)
        return a.transpose(0, 3, 1, 2, 4).reshape(b, B_GROUPS, n_str, hw)

    cmp_end = jnp.arange(n_str) * CMP_STRIDE + (CMP_LEN - 1)
    kc, vc = _compress_call(
        strides(CB_BKC), strides(CB_BVC),
        cmp_pe_k[0].reshape(2, hw), cmp_pe_v[0].reshape(2, hw),
        cmp_w1_k[0].astype(BF16), cmp_w2_k[0].astype(BF16),
        cmp_w1_v[0].astype(BF16), cmp_w2_v[0].astype(BF16),
        _rope_tables(cmp_end, B_HEAD_DIM))
    e_mat = (jnp.arange(s)[:, None] // SLC_BLOCK == jnp.arange(LANE)[None, :]).astype(BF16)
    cmp_start = jnp.arange(n_str) * CMP_STRIDE
    slc_start = jnp.arange(LANE) * SLC_BLOCK
    ovt_mat = ((cmp_start[None, :] < slc_start[:, None] + SLC_BLOCK)
               & (cmp_start[None, :] + CMP_LEN > slc_start[:, None])).astype(BF16)
    ovt_mat = jnp.concatenate([ovt_mat, jnp.ones((SUBLANE, n_str), BF16)], axis=0)
    ob = _nsa_call(p3, kc, vc, e_mat, ovt_mat, _rope_tables(pos, B_HEAD_DIM))

    wbr = w_branch[0].astype(BF16)
    a_w = A_HEADS * A_V_DIM
    out = _out_call(oa.reshape(t, a_w), ob.reshape(t, B_HEADS * B_HEAD_DIM), p2, x2, gate,
                    wbr[:a_w], wbr[a_w:], w_out[0].astype(BF16), final_norm_g[None, :], s)
    return out.reshape(b, s, d)
```

```python
import functools
import math

import jax
import jax.numpy as jnp
from jax import lax
from jax.experimental import pallas as pl
from jax.experimental.pallas import tpu as pltpu

F32 = jnp.float32
BF16 = jnp.bfloat16

D_MODEL = 2048
A_HEADS = 8
A_QK_DIM = 64
A_V_DIM = 128
B_HEADS = 8
B_GROUPS = 2
B_HPG = B_HEADS // B_GROUPS
B_HEAD_DIM = 128
CMP_LEN = 32
CMP_STRIDE = 16
CMP_HIDDEN = 256
SLC_BLOCK = 64
SLC_TOPK = 16
N_FORCED = 3
WINDOW = 512
FORCE_SCORE = 1.0e4
ROPE_THETA = 500000.0
ROPE_FRACTION = 4
EPS = 1e-6
NEG = -1e30
TAKEN = -3.0e38
LAM_INIT = 0.8 - 0.6 * math.exp(-0.3 * 0)
LOG2E = math.log2(math.e)

LANE = 128
SUBLANE = 8
VMEM_LIMIT = 56 * 1024 * 1024

CB_GA, CB_GB, CB_BGATE, CB_MAIN = 0, 16, 32, 36
CB_AQ, CB_AK, CB_AV, CB_AZ, CB_BQ = (CB_MAIN + o for o in (0, 8, 16, 24, 32))
CB_BKC, CB_BVC, CB_BKS, CB_BVS, CB_BKW, CB_BVW, CB_BZ = (CB_MAIN + o for o in (40, 42, 44, 46, 48, 50, 52))
P_WIDTH = 96 * LANE
PROJ_TN = 1536
N_TAIL_TILES = (CB_MAIN * LANE) // PROJ_TN
W_MAIN_COLS = (CB_BZ + 8 - CB_MAIN) * LANE
A_SCALE = LOG2E * A_QK_DIM ** -0.5
B_SCALE = LOG2E * B_HEAD_DIM ** -0.5


def _dot(a, b):
    return jnp.dot(a, b, preferred_element_type=F32)


def _dot_nt(a, b):
    return lax.dot_general(a, b, (((1,), (1,)), ((), ())), preferred_element_type=F32)


def _silu(v):
    return v * jax.nn.sigmoid(v)


def _rope(x, tabs, half):
    c, s1, s2 = tabs
    return x * c + pltpu.roll(x, LANE - half, axis=1) * s1 + pltpu.roll(x, half, axis=1) * s2


def _ada_kernel(c_ref, w_ref, b_ref, o_ref):
    o_ref[...] = _dot(_silu(c_ref[...]), w_ref[...]) + b_ref[...]


def _ada_call(c_pad, w, b):
    rows, d = c_pad.shape
    n = w.shape[1]
    tn = 768
    return pl.pallas_call(
        _ada_kernel,
        grid=(n // tn,),
        in_specs=[pl.BlockSpec((rows, d), lambda j: (0, 0)),
                  pl.BlockSpec((d, tn), lambda j: (0, j)),
                  pl.BlockSpec((1, tn), lambda j: (0, j))],
        out_specs=pl.BlockSpec((rows, tn), lambda j: (0, j)),
        out_shape=jax.ShapeDtypeStruct((rows, n), F32),
        compiler_params=pltpu.CompilerParams(vmem_limit_bytes=VMEM_LIMIT),
        name="ada",
    )(c_pad, w, b)


def _inproj_kernel(x_ref, g_ref, sc_ref, sh_ref, wt_ref, wm_ref, o_ref, h_ref):
    j = pl.program_id(1)

    @pl.when(j == 0)
    def _():
        x = x_ref[...]
        ms = jnp.mean(x * x, axis=-1, keepdims=True)
        xn = x * lax.rsqrt(ms + EPS)
        h_ref[...] = (xn * (g_ref[...] * (1.0 + sc_ref[...])) + sh_ref[...]).astype(h_ref.dtype)

    @pl.when(j < N_TAIL_TILES)
    def _():
        o_ref[...] = _dot(h_ref[...], wt_ref[...]).astype(o_ref.dtype)

    @pl.when(j >= N_TAIL_TILES)
    def _():
        o_ref[...] = _dot(h_ref[...], wm_ref[...]).astype(o_ref.dtype)


def _inproj_call(x2, g, scale, shift, w_tail, w_main, seq):
    t, d = x2.shape
    tm = min(1024, seq)
    per_seq = seq // tm
    mod = pl.BlockSpec((None, 1, d), lambda i, j: (i // per_seq, 0, 0))
    return pl.pallas_call(
        _inproj_kernel,
        grid=(t // tm, P_WIDTH // PROJ_TN),
        in_specs=[pl.BlockSpec((tm, d), lambda i, j: (i, 0)),
                  pl.BlockSpec((1, d), lambda i, j: (0, 0)),
                  mod, mod,
                  pl.BlockSpec((d, PROJ_TN), lambda i, j: (0, jnp.minimum(j, N_TAIL_TILES - 1))),
                  pl.BlockSpec((d, PROJ_TN), lambda i, j: (0, jnp.maximum(j - N_TAIL_TILES, 0)))],
        out_specs=pl.BlockSpec((tm, PROJ_TN), lambda i, j: (i, j)),
        out_shape=jax.ShapeDtypeStruct((t, P_WIDTH), BF16),
        scratch_shapes=[pltpu.VMEM((tm, d), BF16)],
        compiler_params=pltpu.CompilerParams(
            dimension_semantics=("arbitrary", "arbitrary"), vmem_limit_bytes=VMEM_LIMIT),
        name="inproj",
    )(x2, g, scale, shift, w_tail, w_main)


def _flash_init(m_ref, acc_ref):
    m_ref[...] = jnp.full(m_ref.shape, NEG, F32)
    acc_ref[...] = jnp.zeros(acc_ref.shape, F32)


def _with_ones(v):
    return jnp.concatenate([v, jnp.ones(v.shape, v.dtype)], axis=1)


def _flash_update(s, v, m_ref, acc_ref):
    m_old = m_ref[...]
    m_new = jnp.maximum(m_old, jnp.max(s, axis=-1, keepdims=True))
    alpha = jnp.exp2(m_old - m_new)
    p = jnp.exp2(s - jnp.tile(m_new, (1, s.shape[1] // LANE)))
    acc_ref[...] = jnp.tile(alpha, (1, 2)) * acc_ref[...] + _dot(p.astype(v.dtype), _with_ones(v))
    m_ref[...] = m_new


def _flash_result(acc_ref):
    acc = acc_ref[...]
    return acc[:, 0:LANE] / acc[:, LANE:2 * LANE]


def _masked_attend(s, mask, v):
    s = jnp.where(mask, s, NEG)
    p = jnp.exp2(s - jnp.max(s, axis=-1, keepdims=True))
    pv = _dot(p.astype(v.dtype), _with_ones(v))
    return pv[:, 0:LANE] / pv[:, LANE:2 * LANE]


def _diff_kernel(q_ref, k_ref, v_ref, z_ref, c_ref, s1_ref, s2_ref, lq1_ref, lk1_ref, lq2_ref, lk2_ref,
                 g_ref, o_ref, qs_ref, kr_ref, sa_ref, sb_ref, m_ref, acc_ref, *, tq):
    i = pl.program_id(2)
    tabs = (c_ref[...], s1_ref[...], s2_ref[...])
    half = A_QK_DIM // ROPE_FRACTION // 2
    q = _rope(q_ref[...].astype(F32), tabs, half) * A_SCALE
    lane = lax.broadcasted_iota(jnp.int32, q.shape, 1)
    qs_ref[0:tq, :] = jnp.where(lane < A_QK_DIM, q, 0.0).astype(qs_ref.dtype)
    qs_ref[tq:2 * tq, :] = jnp.where(lane >= A_QK_DIM, q, 0.0).astype(qs_ref.dtype)
    kr_ref[pl.ds(pl.multiple_of(i * tq, tq), tq), :] = _rope(k_ref[...].astype(F32), tabs, half).astype(kr_ref.dtype)
    _flash_init(m_ref, acc_ref)

    def scores(j, s_ref):
        s_ref[...] = _dot_nt(qs_ref[...], kr_ref[pl.ds(pl.multiple_of(j * tq, tq), tq), :])

    def update(j, s_ref, causal=False):
        s = s_ref[...]
        if causal:
            row = lax.broadcasted_iota(jnp.int32, s.shape, 0)
            col = lax.broadcasted_iota(jnp.int32, s.shape, 1)
            s = jnp.where(col <= (row & (tq - 1)), s, NEG)
        _flash_update(s, v_ref[pl.ds(pl.multiple_of(j * tq, tq), tq), :], m_ref, acc_ref)

    scores(0, sa_ref)

    def pair(jj, carry):
        j = 2 * jj
        scores(j + 1, sb_ref)
        update(j, sa_ref)
        scores(j + 2, sa_ref)
        update(j + 1, sb_ref)
        return carry

    lax.fori_loop(0, i // 2, pair, 0)

    @pl.when(i % 2 == 1)
    def _():
        scores(i, sb_ref)
        update(i - 1, sa_ref)
        update(i, sb_ref, causal=True)

    @pl.when(i % 2 == 0)
    def _():
        update(i, sa_ref, causal=True)

    o = _flash_result(acc_ref)
    lam = (jnp.exp(jnp.sum(lq1_ref[...] * lk1_ref[...], axis=-1, keepdims=True))
           - jnp.exp(jnp.sum(lq2_ref[...] * lk2_ref[...], axis=-1, keepdims=True)) + LAM_INIT)
    d = o[0:tq] - lam * o[tq:2 * tq]
    dn = d * lax.rsqrt(jnp.mean(d * d, axis=-1, keepdims=True) + EPS)
    dn = dn * g_ref[...] * (1.0 - LAM_INIT)
    o_ref[...] = (dn * _silu(z_ref[...].astype(F32))).astype(o_ref.dtype)


def _diff_call(p3, tabs, lq1, lk1, lq2, lk2, sub_g):
    b, s, _ = p3.shape
    tq = 512
    lam_spec = pl.BlockSpec((1, A_QK_DIM), lambda bi, h, i: (0, 0))
    tab_spec = pl.BlockSpec((tq, LANE), lambda bi, h, i: (i, 0))
    return pl.pallas_call(
        functools.partial(_diff_kernel, tq=tq),
        grid=(b, A_HEADS, s // tq),
        in_specs=[pl.BlockSpec((None, tq, LANE), lambda bi, h, i: (bi, i, CB_AQ + h)),
                  pl.BlockSpec((None, tq, LANE), lambda bi, h, i: (bi, i, CB_AK + h)),
                  pl.BlockSpec((None, s, LANE), lambda bi, h, i: (bi, 0, CB_AV + h)),
                  pl.BlockSpec((None, tq, LANE), lambda bi, h, i: (bi, i, CB_AZ + h)),
                  tab_spec, tab_spec, tab_spec,
                  lam_spec, lam_spec, lam_spec, lam_spec,
                  pl.BlockSpec((1, A_V_DIM), lambda bi, h, i: (0, 0))],
        out_specs=pl.BlockSpec((None, tq, LANE), lambda bi, h, i: (bi, i, h)),
        out_shape=jax.ShapeDtypeStruct((b, s, A_HEADS * A_V_DIM), BF16),
        scratch_shapes=[pltpu.VMEM((2 * tq, LANE), BF16),
                        pltpu.VMEM((s, LANE), BF16),
                        pltpu.VMEM((2 * tq, tq), F32),
                        pltpu.VMEM((2 * tq, tq), F32),
                        pltpu.VMEM((2 * tq, LANE), F32),
                        pltpu.VMEM((2 * tq, 2 * LANE), F32)],
        compiler_params=pltpu.CompilerParams(
            dimension_semantics=("arbitrary", "arbitrary", "arbitrary"), vmem_limit_bytes=VMEM_LIMIT),
        name="diff_attn",
    )(p3, p3, p3, p3, *tabs, lq1, lk1, lq2, lk2, sub_g)


def _compress_kernel(xk_ref, xv_ref, pek_ref, pev_ref, w1k_ref, w2k_ref, w1v_ref, w2v_ref,
                     c_ref, s1_ref, s2_ref, kc_ref, vc_ref):
    n = xk_ref.shape[0]
    hd = B_HEAD_DIM
    half_w = CMP_STRIDE * hd
    tabs = (c_ref[...], s1_ref[...], s2_ref[...])

    def mlp(x_ref, g, pe_ref, w1_ref, w2_ref):
        x = jnp.concatenate([x_ref[:, (r * B_GROUPS + g) * hd:(r * B_GROUPS + g + 1) * hd]
                             for r in range(CMP_STRIDE)], axis=1).astype(F32)
        top = _dot((x + pe_ref[0:1, :]).astype(BF16), w1_ref[0:half_w, :])
        bot = _dot((x + pe_ref[1:2, :]).astype(BF16), w1_ref[half_w:2 * half_w, :])
        hid = top + pltpu.roll(bot, n - 1, axis=0)
        return _dot(_silu(hid).astype(BF16), w2_ref[...])

    for g in range(B_GROUPS):
        kc = mlp(xk_ref, g, pek_ref, w1k_ref, w2k_ref)
        kc_ref[g] = _rope(kc, tabs, hd // ROPE_FRACTION // 2).astype(kc_ref.dtype)
        vc_ref[g] = mlp(xv_ref, g, pev_ref, w1v_ref, w2v_ref).astype(vc_ref.dtype)


def _compress_call(xk, xv, pek, pev, w1k, w2k, w1v, w2v, tabs):
    b, n, xw = xk.shape
    x_spec = pl.BlockSpec((None, n, xw), lambda bi: (bi, 0, 0))
    o_spec = pl.BlockSpec((None, B_GROUPS, n, B_HEAD_DIM), lambda bi: (bi, 0, 0, 0))

    def full(a):
        return pl.BlockSpec(a.shape, lambda bi: (0,) * a.ndim)

    consts = (pek, pev, w1k, w2k, w1v, w2v, *tabs)
    return pl.pallas_call(
        _compress_kernel,
        grid=(b,),
        in_specs=[x_spec, x_spec] + [full(a) for a in consts],
        out_specs=[o_spec, o_spec],
        out_shape=[jax.ShapeDtypeStruct((b, B_GROUPS, n, B_HEAD_DIM), BF16)] * 2,
        compiler_params=pltpu.CompilerParams(
            dimension_semantics=("arbitrary",), vmem_limit_bytes=VMEM_LIMIT),
        name="compress",
    )(xk, xv, *consts)


def _select_blocks(imp_t, s0, n_slc, n_top):
    shape = imp_t.shape
    jb = lax.broadcasted_iota(jnp.int32, shape, 0)
    jbf = jb.astype(F32)
    qpos = s0 + lax.broadcasted_iota(jnp.int32, shape, 1)
    cur = qpos // SLC_BLOCK
    valid = (jb * SLC_BLOCK <= qpos) & (jb < n_slc)
    forced = (jb == 0) | (jb == cur) | (jb == cur - 1)
    imp = jnp.where(valid, jnp.where(forced, TAKEN, imp_t), NEG)
    for _ in range(n_top - N_FORCED):
        mx = jnp.max(imp, axis=0, keepdims=True)
        idx = jnp.min(jnp.where(imp == mx, jbf, float(LANE)), axis=0, keepdims=True)
        imp = jnp.where(jbf == idx, TAKEN, imp)
    return jnp.where((imp == TAKEN) & valid, 0.0, NEG)


def _nsa_kernel(q_ref, kc_ref, vc_ref, ks_ref, vs_ref, kw_ref, vw_ref, e_ref, ovt_ref, gate_ref, z_ref,
                c_ref, s1_ref, s2_ref, o_ref, qa_ref, ksr_ref, kwr_ref, sa_ref, sb_ref, m_ref, acc_ref,
                *, tq, tk, n_slc, n_top):
    i = pl.program_id(2)
    s0 = i * tq
    hd = B_HEAD_DIM
    rows = B_HPG * tq
    half = hd // ROPE_FRACTION // 2

    sub = pl.multiple_of((i % (tk // tq)) * tq, tq)
    qtabs = tuple(r[pl.ds(sub, tq), :] for r in (c_ref, s1_ref, s2_ref))
    qh = [(_rope(q_ref[:, h * hd:(h + 1) * hd].astype(F32), qtabs, half) * B_SCALE).astype(BF16)
          for h in range(B_HPG)]
    qs = jnp.concatenate(qh, axis=0)

    @pl.when(i % (tk // tq) == 0)
    def _():
        ktabs = (c_ref[...], s1_ref[...], s2_ref[...])
        k0 = pl.multiple_of(s0, tk)
        ksr_ref[pl.ds(k0, tk), :] = _rope(ks_ref[...].astype(F32), ktabs, half).astype(BF16)
        kwr_ref[pl.ds(k0, tk), :] = _rope(kw_ref[...].astype(F32), ktabs, half).astype(BF16)

    rid = lax.broadcasted_iota(jnp.int32, (rows, 1), 0)
    qpos_r = s0 + (rid & (tq - 1))

    n_cmp_pad = kc_ref.shape[0]
    cend = lax.broadcasted_iota(jnp.int32, (1, n_cmp_pad), 1) * CMP_STRIDE + (CMP_LEN - 1)
    sc = jnp.where(cend <= qpos_r, _dot_nt(qs, kc_ref[...]), NEG)
    pc = jnp.exp2(sc - jnp.max(sc, axis=-1, keepdims=True)).astype(BF16)
    pv = _dot(pc, _with_ones(vc_ref[...]))
    o_cmp = pv[:, 0:LANE] * jnp.where(qpos_r >= CMP_LEN - 1, 1.0 / pv[:, LANE:2 * LANE], 0.0)
    qpos_t = s0 + lax.broadcasted_iota(jnp.int32, (1, tq), 1)
    imp_t = jnp.zeros((LANE, tq), F32)
    for h in range(B_HPG):
        r = _dot_nt(ovt_ref[...], pc[h * tq:(h + 1) * tq])
        imp_t = imp_t + r[0:LANE] * jnp.where(qpos_t >= CMP_LEN - 1, 1.0 / r[LANE:LANE + 1], 0.0)

    wlen = WINDOW + tq
    w0 = pl.multiple_of(jnp.maximum(s0 - WINDOW, 0), tq)
    s_win = _dot_nt(qs, kwr_ref[pl.ds(w0, wlen), :])

    selb16 =_select_blocks(imp_t, s0, n_slc, n_top).T.astype(BF16)
    for h in range(B_HPG):
        qa_ref[h * tq:(h + 1) * tq, 0:hd] = qh[h]
        qa_ref[h * tq:(h + 1) * tq, hd:2 * hd] = selb16
    _flash_init(m_ref, acc_ref)

    def slc_scores(j, s_ref):
        k0 = pl.multiple_of(j * tk, tk)
        ka = jnp.concatenate([ksr_ref[pl.ds(k0, tk), :], e_ref[pl.ds(k0, tk), :]], axis=1)
        s_ref[...] = _dot_nt(qa_ref[...], ka)

    def slc_update(j, s_ref, causal=False):
        k0 = pl.multiple_of(j * tk, tk)
        s = s_ref[...]
        if causal:
            kpos = k0 + lax.broadcasted_iota(jnp.int32, (1, tk), 1)
            s = jnp.where(kpos <= qpos_r, s, NEG)
        _flash_update(s, vs_ref[pl.ds(k0, tk), :], m_ref, acc_ref)

    slc_scores(0, sa_ref)

    wpos = w0 + lax.broadcasted_iota(jnp.int32, (1, wlen), 1)
    wmask = (wpos <= qpos_r) & (wpos > qpos_r - WINDOW)
    o_win = _masked_attend(s_win, wmask, vw_ref[pl.ds(w0, wlen), :])

    def pair(jj, carry):
        j = 2 * jj
        slc_scores(j + 1, sb_ref)
        slc_update(j, sa_ref)
        slc_scores(j + 2, sa_ref)
        slc_update(j + 1, sb_ref)
        return carry

    jd = s0 // tk
    lax.fori_loop(0, jd // 2, pair, 0)

    @pl.when(jd % 2 == 1)
    def _():
        slc_scores(jd, sb_ref)
        slc_update(jd - 1, sa_ref)
        slc_update(jd, sb_ref, causal=True)

    @pl.when(jd % 2 == 0)
    def _():
        slc_update(jd, sa_ref, causal=True)

    o_slc = _flash_result(acc_ref)

    gates = jax.nn.sigmoid(gate_ref[...].astype(F32))
    for h in range(B_HPG):
        r = slice(h * tq, (h + 1) * tq)
        o = (gates[:, 3 * h:3 * h + 1] * o_cmp[r] + gates[:, 3 * h + 1:3 * h + 2] * o_slc[r]
             + gates[:, 3 * h + 2:3 * h + 3] * o_win[r])
        z = z_ref[:, h * hd:(h + 1) * hd].astype(F32)
        o_ref[:, h * hd:(h + 1) * hd] = (o * _silu(z)).astype(o_ref.dtype)


def _nsa_call(p3, kc, vc, e_mat, ovt_mat, tabs):
    b, s, _ = p3.shape
    tq, tk = 256, 512
    n_slc = s // SLC_BLOCK
    n_top = min(SLC_TOPK, n_slc)
    gw = B_HPG * B_HEAD_DIM
    rows = B_HPG * tq
    n_cmp_pad = kc.shape[2]
    per_k = tk // tq

    def slab(cb):
        return pl.BlockSpec((None, s, LANE), lambda bi, g, i: (bi, 0, cb + g))

    def ktile(cb):
        return pl.BlockSpec((None, tk, LANE), lambda bi, g, i: (bi, i // per_k, cb + g))

    cmp_spec = pl.BlockSpec((None, None, n_cmp_pad, B_HEAD_DIM), lambda bi, g, i: (bi, g, 0, 0))
    tab_spec = pl.BlockSpec((tk, LANE), lambda bi, g, i: (i // per_k, 0))
    return pl.pallas_call(
        functools.partial(_nsa_kernel, tq=tq, tk=tk, n_slc=n_slc, n_top=n_top),
        grid=(b, B_GROUPS, s // tq),
        in_specs=[pl.BlockSpec((None, tq, gw), lambda bi, g, i: (bi, i, CB_BQ * LANE // gw + g)),
                  cmp_spec, cmp_spec,
                  ktile(CB_BKS), slab(CB_BVS), ktile(CB_BKW), slab(CB_BVW),
                  pl.BlockSpec((s, LANE), lambda bi, g, i: (0, 0)),
                  pl.BlockSpec((LANE + SUBLANE, n_cmp_pad), lambda bi, g, i: (0, 0)),
                  pl.BlockSpec((None, tq, LANE), lambda bi, g, i: (bi, i, CB_BGATE + g)),
                  pl.BlockSpec((None, tq, gw), lambda bi, g, i: (bi, i, CB_BZ * LANE // gw + g)),
                  tab_spec, tab_spec, tab_spec],
        out_specs=pl.BlockSpec((None, tq, gw), lambda bi, g, i: (bi, i, g)),
        out_shape=jax.ShapeDtypeStruct((b, s, B_HEADS * B_HEAD_DIM), BF16),
        scratch_shapes=[pltpu.VMEM((rows, 2 * B_HEAD_DIM), BF16),
                        pltpu.VMEM((s, LANE), BF16),
                        pltpu.VMEM((s, LANE), BF16),
                        pltpu.VMEM((rows, tk), F32),
                        pltpu.VMEM((rows, tk), F32),
                        pltpu.VMEM((rows, LANE), F32),
                        pltpu.VMEM((rows, 2 * LANE), F32)],
        compiler_params=pltpu.CompilerParams(
            dimension_semantics=("arbitrary", "arbitrary", "arbitrary"), vmem_limit_bytes=VMEM_LIMIT),
        name="nsa",
    )(p3, kc, vc, p3, p3, p3, p3, e_mat, ovt_mat, p3, p3, *tabs)


def _out_kernel(oa_ref, ob_ref, ga_ref, gb_ref, x_ref, gate_ref, wa_ref, wb_ref, wo_ref, fg_ref, o_ref):
    ya = _dot(oa_ref[...], wa_ref[...])
    yb = _dot(ob_ref[...], wb_ref[...])
    mix = (jax.nn.sigmoid(ga_ref[...].astype(F32)) * ya
           + jax.nn.sigmoid(gb_ref[...].astype(F32)) * yb)
    y = _dot(mix.astype(BF16), wo_ref[...])
    xo = x_ref[...] + gate_ref[...] * y
    o_ref[...] = xo * lax.rsqrt(jnp.mean(xo * xo, axis=-1, keepdims=True) + EPS) * fg_ref[...]


def _out_call(oa, ob, p2, x2, gate, wa, wb, wo, fg, seq):
    t, d = x2.shape
    tm = 256
    per_seq = seq // tm
    aw = oa.shape[1]
    bw = ob.shape[1]

    def resident(a):
        return pl.BlockSpec(a.shape, lambda i: (0, 0), pipeline_mode=pl.Buffered(1))

    return pl.pallas_call(
        _out_kernel,
        grid=(t // tm,),
        in_specs=[pl.BlockSpec((tm, aw), lambda i: (i, 0)),
                  pl.BlockSpec((tm, bw), lambda i: (i, 0)),
                  pl.BlockSpec((tm, d), lambda i: (i, CB_GA * LANE // d)),
                  pl.BlockSpec((tm, d), lambda i: (i, CB_GB * LANE // d)),
                  pl.BlockSpec((tm, d), lambda i: (i, 0)),
                  pl.BlockSpec((None, 1, d), lambda i: (i // per_seq, 0, 0)),
                  resident(wa), resident(wb), resident(wo),
                  pl.BlockSpec((1, d), lambda i: (0, 0))],
        out_specs=pl.BlockSpec((tm, d), lambda i: (i, 0)),
        out_shape=jax.ShapeDtypeStruct((t, d), F32),
        compiler_params=pltpu.CompilerParams(
            dimension_semantics=("arbitrary",), vmem_limit_bytes=VMEM_LIMIT),
        name="out_proj",
    )(oa, ob, p2, p2, x2, gate, wa, wb, wo, fg)


def _rope_tables(pos, head_dim):
    rd = head_dim // ROPE_FRACTION
    half = rd // 2
    inv = 1.0 / (ROPE_THETA ** (jnp.arange(half, dtype=F32) * (2.0 / rd)))
    ang = pos.astype(F32)[:, None] * inv[None, :]
    cos, sin = jnp.cos(ang), jnp.sin(ang)
    n = pos.shape[0]
    pad = jnp.zeros((n, head_dim - rd), F32)
    zero = jnp.zeros((n, half), F32)
    c = jnp.concatenate([cos, cos, pad + 1.0], axis=1)
    s1 = jnp.concatenate([-sin, zero, pad], axis=1)
    s2 = jnp.concatenate([zero, sin, pad], axis=1)
    reps = LANE // head_dim
    return tuple(jnp.tile(a, (1, reps)) for a in (c, s1, s2))


def _gate_weights(w):
    d = w.shape[0]
    gpg = B_HPG * 3
    a_g = W_MAIN_COLS
    a_m = a_g + B_GROUPS * gpg
    zpad = jnp.zeros((d, LANE - gpg), w.dtype)
    return jnp.concatenate([w[:, a_m:a_m + 2 * D_MODEL],
                            w[:, a_g:a_g + gpg], zpad, w[:, a_g + gpg:a_g + 2 * gpg], zpad,
                            jnp.zeros((d, 2 * LANE), w.dtype)], axis=1)


def kernel(x, c, w_ada, b_ada, norm_g, w_in, lambda_q1, lambda_k1, lambda_q2, lambda_k2, diff_norm_g,
           cmp_pe_k, cmp_pe_v, cmp_w1_k, cmp_w2_k, cmp_w1_v, cmp_w2_v, w_branch, w_out, final_norm_g):
    b, s, d = x.shape
    assert d == D_MODEL and s % 1024 == 0 and s // SLC_BLOCK <= LANE
    t = b * s
    x2 = x.reshape(t, d)

    c_pad = jnp.pad(c, ((0, 8 - b % 8 if b % 8 else 0), (0, 0)))
    mod = _ada_call(c_pad, w_ada[0], b_ada[0][None, :])[:b]
    shift = mod[:, None, 0:d]
    scale = mod[:, None, d:2 * d]
    gate = mod[:, None, 2 * d:3 * d]

    w_bf = w_in[0].astype(BF16)
    p2 = _inproj_call(x2, norm_g[0][None, :], scale, shift, _gate_weights(w_bf), w_bf, s)
    p3 = p2.reshape(b, s, P_WIDTH)

    pos = jnp.arange(s)
    oa = _diff_call(p3, _rope_tables(pos, A_QK_DIM), lambda_q1, lambda_k1, lambda_q2, lambda_k2, diff_norm_g)

    n_str = s // CMP_STRIDE
    hw = CMP_STRIDE * B_HEAD_DIM

    def strides(cb):
        return p3[:, :, cb * LANE:(cb + B_GROUPS) * LANE].reshape(b, n_str, CMP_STRIDE * B_GROUPS * B_HEAD_DIM)

    cmp_end = jnp.arange(n_str) * CMP_STRIDE + (CMP_LEN - 1)
    kc, vc = _compress_call(
        strides(CB_BKC), strides(CB_BVC),
        cmp_pe_k[0].reshape(2, hw), cmp_pe_v[0].reshape(2, hw),
        cmp_w1_k[0].astype(BF16), cmp_w2_k[0].astype(BF16),
        cmp_w1_v[0].astype(BF16), cmp_w2_v[0].astype(BF16),
        _rope_tables(cmp_end, B_HEAD_DIM))
    e_mat = (jnp.arange(s)[:, None] // SLC_BLOCK == jnp.arange(LANE)[None, :]).astype(BF16)
    cmp_start = jnp.arange(n_str) * CMP_STRIDE
    slc_start = jnp.arange(LANE) * SLC_BLOCK
    ovt_mat = ((cmp_start[None, :] < slc_start[:, None] + SLC_BLOCK)
               & (cmp_start[None, :] + CMP_LEN > slc_start[:, None])).astype(BF16)
    ovt_mat = jnp.concatenate([ovt_mat, jnp.ones((SUBLANE, n_str), BF16)], axis=0)
    ob = _nsa_call(p3, kc, vc, e_mat, ovt_mat, _rope_tables(pos, B_HEAD_DIM))

    wbr = w_branch[0].astype(BF16)
    a_w = A_HEADS * A_V_DIM
    out = _out_call(oa.reshape(t, a_w), ob.reshape(t, B_HEADS * B_HEAD_DIM), p2, x2, gate,
                    wbr[:a_w], wbr[a_w:], w_out[0].astype(BF16), final_norm_g[None, :], s)
    return out.reshape(b, s, d)
```

```python
import functools
import math

import jax
import jax.numpy as jnp
from jax import lax
from jax.experimental import pallas as pl
from jax.experimental.pallas import tpu as pltpu

F32 = jnp.float32
BF16 = jnp.bfloat16

D_MODEL = 2048
A_HEADS = 8
A_QK_DIM = 64
A_V_DIM = 128
B_HEADS = 8
B_GROUPS = 2
B_HPG = B_HEADS // B_GROUPS
B_HEAD_DIM = 128
CMP_LEN = 32
CMP_STRIDE = 16
CMP_HIDDEN = 256
SLC_BLOCK = 64
SLC_TOPK = 16
N_FORCED = 3
WINDOW = 512
FORCE_SCORE = 1.0e4
ROPE_THETA = 500000.0
ROPE_FRACTION = 4
EPS = 1e-6
NEG = -1e30
TAKEN = -3.0e38
LAM_INIT = 0.8 - 0.6 * math.exp(-0.3 * 0)
LOG2E = math.log2(math.e)

LANE = 128
SUBLANE = 8
VMEM_LIMIT = 56 * 1024 * 1024

CB_GA, CB_GB, CB_BGATE, CB_MAIN = 0, 16, 32, 36
CB_AQ, CB_AK, CB_AV, CB_AZ, CB_BQ = (CB_MAIN + o for o in (0, 8, 16, 24, 32))
CB_BKC, CB_BVC, CB_BKS, CB_BVS, CB_BKW, CB_BVW, CB_BZ = (CB_MAIN + o for o in (40, 42, 44, 46, 48, 50, 52))
P_WIDTH = 96 * LANE
PROJ_TN = 1536
N_TAIL_TILES = (CB_MAIN * LANE) // PROJ_TN
W_MAIN_COLS = (CB_BZ + 8 - CB_MAIN) * LANE
A_SCALE = LOG2E * A_QK_DIM ** -0.5
B_SCALE = LOG2E * B_HEAD_DIM ** -0.5


def _dot(a, b):
    return jnp.dot(a, b, preferred_element_type=F32)


def _dot_nt(a, b):
    return lax.dot_general(a, b, (((1,), (1,)), ((), ())), preferred_element_type=F32)


def _silu(v):
    return v * jax.nn.sigmoid(v)


def _rope(x, tabs, half):
    c, s1, s2 = tabs
    return x * c + pltpu.roll(x, LANE - half, axis=1) * s1 + pltpu.roll(x, half, axis=1) * s2


def _ada_kernel(c_ref, w_ref, b_ref, o_ref):
    o_ref[...] = _dot(_silu(c_ref[...]), w_ref[...]) + b_ref[...]


def _ada_call(c_pad, w, b):
    rows, d = c_pad.shape
    n = w.shape[1]
    tn = 768
    return pl.pallas_call(
        _ada_kernel,
        grid=(n // tn,),
        in_specs=[pl.BlockSpec((rows, d), lambda j: (0, 0)),
                  pl.BlockSpec((d, tn), lambda j: (0, j)),
                  pl.BlockSpec((1, tn), lambda j: (0, j))],
        out_specs=pl.BlockSpec((rows, tn), lambda j: (0, j)),
        out_shape=jax.ShapeDtypeStruct((rows, n), F32),
        compiler_params=pltpu.CompilerParams(vmem_limit_bytes=VMEM_LIMIT),
        name="ada",
    )(c_pad, w, b)


def _inproj_kernel(x_ref, g_ref, sc_ref, sh_ref, wt_ref, wm_ref, o_ref, h_ref):
    j = pl.program_id(1)

    @pl.when(j == 0)
    def _():
        x = x_ref[...]
        ms = jnp.mean(x * x, axis=-1, keepdims=True)
        xn = x * lax.rsqrt(ms + EPS)
        h_ref[...] = (xn * (g_ref[...] * (1.0 + sc_ref[...])) + sh_ref[...]).astype(h_ref.dtype)

    @pl.when(j < N_TAIL_TILES)
    def _():
        o_ref[...] = _dot(h_ref[...], wt_ref[...]).astype(o_ref.dtype)

    @pl.when(j >= N_TAIL_TILES)
    def _():
        o_ref[...] = _dot(h_ref[...], wm_ref[...]).astype(o_ref.dtype)


def _inproj_call(x2, g, scale, shift, w_tail, w_main, seq):
    t, d = x2.shape
    tm = min(1024, seq)
    per_seq = seq // tm
    mod = pl.BlockSpec((None, 1, d), lambda i, j: (i // per_seq, 0, 0))
    return pl.pallas_call(
        _inproj_kernel,
        grid=(t // tm, P_WIDTH // PROJ_TN),
        in_specs=[pl.BlockSpec((tm, d), lambda i, j: (i, 0)),
                  pl.BlockSpec((1, d), lambda i, j: (0, 0)),
                  mod, mod,
                  pl.BlockSpec((d, PROJ_TN), lambda i, j: (0, jnp.minimum(j, N_TAIL_TILES - 1))),
                  pl.BlockSpec((d, PROJ_TN), lambda i, j: (0, jnp.maximum(j - N_TAIL_TILES, 0)))],
        out_specs=pl.BlockSpec((tm, PROJ_TN), lambda i, j: (i, j)),
        out_shape=jax.ShapeDtypeStruct((t, P_WIDTH), BF16),
        scratch_shapes=[pltpu.VMEM((tm, d), BF16)],
        compiler_params=pltpu.CompilerParams(
            dimension_semantics=("arbitrary", "arbitrary"), vmem_limit_bytes=VMEM_LIMIT),
        name="inproj",
    )(x2, g, scale, shift, w_tail, w_main)


def _flash_init(m_ref, acc_ref):
    m_ref[...] = jnp.full(m_ref.shape, NEG, F32)
    acc_ref[...] = jnp.zeros(acc_ref.shape, F32)


def _with_ones(v):
    return jnp.concatenate([v, jnp.ones(v.shape, v.dtype)], axis=1)


def _flash_update(s, v, m_ref, acc_ref):
    m_old = m_ref[...]
    m_new = jnp.maximum(m_old, jnp.max(s, axis=-1, keepdims=True))
    alpha = jnp.exp2(m_old - m_new)
    p = jnp.exp2(s - jnp.tile(m_new, (1, s.shape[1] // LANE)))
    acc_ref[...] = jnp.tile(alpha, (1, 2)) * acc_ref[...] + _dot(p.astype(v.dtype), _with_ones(v))
    m_ref[...] = m_new


def _flash_result(acc_ref):
    acc = acc_ref[...]
    return acc[:, 0:LANE] / acc[:, LANE:2 * LANE]


def _masked_attend(s, mask, v):
    s = jnp.where(mask, s, NEG)
    p = jnp.exp2(s - jnp.max(s, axis=-1, keepdims=True))
    pv = _dot(p.astype(v.dtype), _with_ones(v))
    return pv[:, 0:LANE] / pv[:, LANE:2 * LANE]


def _diff_kernel(q_ref, k_ref, v_ref, z_ref, c_ref, s1_ref, s2_ref, lq1_ref, lk1_ref, lq2_ref, lk2_ref,
                 g_ref, o_ref, qs_ref, kr_ref, sa_ref, sb_ref, m_ref, acc_ref, *, tq):
    i = pl.program_id(2)
    tabs = (c_ref[...], s1_ref[...], s2_ref[...])
    half = A_QK_DIM // ROPE_FRACTION // 2
    q = _rope(q_ref[...].astype(F32), tabs, half) * A_SCALE
    lane = lax.broadcasted_iota(jnp.int32, q.shape, 1)
    qs_ref[0:tq, :] = jnp.where(lane < A_QK_DIM, q, 0.0).astype(qs_ref.dtype)
    qs_ref[tq:2 * tq, :] = jnp.where(lane >= A_QK_DIM, q, 0.0).astype(qs_ref.dtype)
    kr_ref[pl.ds(pl.multiple_of(i * tq, tq), tq), :] = _rope(k_ref[...].astype(F32), tabs, half).astype(kr_ref.dtype)
    _flash_init(m_ref, acc_ref)

    def scores(j, s_ref):
        s_ref[...] = _dot_nt(qs_ref[...], kr_ref[pl.ds(pl.multiple_of(j * tq, tq), tq), :])

    def update(j, s_ref, causal=False):
        s = s_ref[...]
        if causal:
            row = lax.broadcasted_iota(jnp.int32, s.shape, 0)
            col = lax.broadcasted_iota(jnp.int32, s.shape, 1)
            s = jnp.where(col <= (row & (tq - 1)), s, NEG)
        _flash_update(s, v_ref[pl.ds(pl.multiple_of(j * tq, tq), tq), :], m_ref, acc_ref)

    scores(0, sa_ref)

    def pair(jj, carry):
        j = 2 * jj
        scores(j + 1, sb_ref)
        update(j, sa_ref)
        scores(j + 2, sa_ref)
        update(j + 1, sb_ref)
        return carry

    lax.fori_loop(0, i // 2, pair, 0)

    @pl.when(i % 2 == 1)
    def _():
        scores(i, sb_ref)
        update(i - 1, sa_ref)
        update(i, sb_ref, causal=True)

    @pl.when(i % 2 == 0)
    def _():
        update(i, sa_ref, causal=True)

    o = _flash_result(acc_ref)
    lam = (jnp.exp(jnp.sum(lq1_ref[...] * lk1_ref[...], axis=-1, keepdims=True))
           - jnp.exp(jnp.sum(lq2_ref[...] * lk2_ref[...], axis=-1, keepdims=True)) + LAM_INIT)
    d = o[0:tq] - lam * o[tq:2 * tq]
    dn = d * lax.rsqrt(jnp.mean(d * d, axis=-1, keepdims=True) + EPS)
    dn = dn * g_ref[...] * (1.0 - LAM_INIT)
    o_ref[...] = (dn * _silu(z_ref[...].astype(F32))).astype(o_ref.dtype)


def _diff_call(p3, tabs, lq1, lk1, lq2, lk2, sub_g):
    b, s, _ = p3.shape
    tq = 512
    lam_spec = pl.BlockSpec((1, A_QK_DIM), lambda bi, h, i: (0, 0))
    tab_spec = pl.BlockSpec((tq, LANE), lambda bi, h, i: (i, 0))
    return pl.pallas_call(
        functools.partial(_diff_kernel, tq=tq),
        grid=(b, A_HEADS, s // tq),
        in_specs=[pl.BlockSpec((None, tq, LANE), lambda bi, h, i: (bi, i, CB_AQ + h)),
                  pl.BlockSpec((None, tq, LANE), lambda bi, h, i: (bi, i, CB_AK + h)),
                  pl.BlockSpec((None, s, LANE), lambda bi, h, i: (bi, 0, CB_AV + h)),
                  pl.BlockSpec((None, tq, LANE), lambda bi, h, i: (bi, i, CB_AZ + h)),
                  tab_spec, tab_spec, tab_spec,
                  lam_spec, lam_spec, lam_spec, lam_spec,
                  pl.BlockSpec((1, A_V_DIM), lambda bi, h, i: (0, 0))],
        out_specs=pl.BlockSpec((None, tq, LANE), lambda bi, h, i: (bi, i, h)),
        out_shape=jax.ShapeDtypeStruct((b, s, A_HEADS * A_V_DIM), BF16),
        scratch_shapes=[pltpu.VMEM((2 * tq, LANE), BF16),
                        pltpu.VMEM((s, LANE), BF16),
                        pltpu.VMEM((2 * tq, tq), F32),
                        pltpu.VMEM((2 * tq, tq), F32),
                        pltpu.VMEM((2 * tq, LANE), F32),
                        pltpu.VMEM((2 * tq, 2 * LANE), F32)],
        compiler_params=pltpu.CompilerParams(
            dimension_semantics=("arbitrary", "arbitrary", "arbitrary"), vmem_limit_bytes=VMEM_LIMIT),
        name="diff_attn",
    )(p3, p3, p3, p3, *tabs, lq1, lk1, lq2, lk2, sub_g)


def _compress_kernel(xk_ref, xv_ref, pek_ref, pev_ref, w1k_ref, w2k_ref, w1v_ref, w2v_ref,
                     c_ref, s1_ref, s2_ref, kc_ref, vc_ref):
    n = xk_ref.shape[0]
    hd = B_HEAD_DIM
    half_w = CMP_STRIDE * hd
    tabs = (c_ref[...], s1_ref[...], s2_ref[...])

    def mlp(x_ref, g, pe_ref, w1_ref, w2_ref):
        x = jnp.concatenate([x_ref[:, (r * B_GROUPS + g) * hd:(r * B_GROUPS + g + 1) * hd]
                             for r in range(CMP_STRIDE)], axis=1).astype(F32)
        top = _dot((x + pe_ref[0:1, :]).astype(BF16), w1_ref[0:half_w, :])
        bot = _dot((x + pe_ref[1:2, :]).astype(BF16), w1_ref[half_w:2 * half_w, :])
        hid = top + pltpu.roll(bot, n - 1, axis=0)
        return _dot(_silu(hid).astype(BF16), w2_ref[...])

    for g in range(B_GROUPS):
        kc = mlp(xk_ref, g, pek_ref, w1k_ref, w2k_ref)
        kc_ref[g] = _rope(kc, tabs, hd // ROPE_FRACTION // 2).astype(kc_ref.dtype)
        vc_ref[g] = mlp(xv_ref, g, pev_ref, w1v_ref, w2v_ref).astype(vc_ref.dtype)


def _compress_call(xk, xv, pek, pev, w1k, w2k, w1v, w2v, tabs):
    b, n, xw = xk.shape
    x_spec = pl.BlockSpec((None, n, xw), lambda bi: (bi, 0, 0))
    o_spec = pl.BlockSpec((None, B_GROUPS, n, B_HEAD_DIM), lambda bi: (bi, 0, 0, 0))

    def full(a):
        return pl.BlockSpec(a.shape, lambda bi: (0,) * a.ndim)

    consts = (pek, pev, w1k, w2k, w1v, w2v, *tabs)
    return pl.pallas_call(
        _compress_kernel,
        grid=(b,),
        in_specs=[x_spec, x_spec] + [full(a) for a in consts],
        out_specs=[o_spec, o_spec],
        out_shape=[jax.ShapeDtypeStruct((b, B_GROUPS, n, B_HEAD_DIM), BF16)] * 2,
        compiler_params=pltpu.CompilerParams(
            dimension_semantics=("arbitrary",), vmem_limit_bytes=VMEM_LIMIT),
        name="compress",
    )(xk, xv, *consts)


def _select_blocks(imp_t, s0, n_slc, n_top):
    shape = imp_t.shape
    jb = lax.broadcasted_iota(jnp.int32, shape, 0)
    jbf = jb.astype(F32)
    qpos = s0 + lax.broadcasted_iota(jnp.int32, shape, 1)
    cur = qpos // SLC_BLOCK
    valid = (jb * SLC_BLOCK <= qpos) & (jb < n_slc)
    forced = (jb == 0) | (jb == cur) | (jb == cur - 1)
    imp = jnp.where(valid, jnp.where(forced, TAKEN, imp_t), NEG)
    for _ in range(n_top - N_FORCED):
        mx = jnp.max(imp, axis=0, keepdims=True)
        idx = jnp.min(jnp.where(imp == mx, jbf, float(LANE)), axis=0, keepdims=True)
        imp = jnp.where(jbf == idx, TAKEN, imp)
    return jnp.where((imp == TAKEN) & valid, 0.0, NEG)


def _nsa_kernel(q_ref, kc_ref, vc_ref, ks_ref, vs_ref, kw_ref, vw_ref, e_ref, ovt_ref, gate_ref, z_ref,
                c_ref, s1_ref, s2_ref, o_ref, qa_ref, ksr_ref, kwr_ref, sa_ref, sb_ref, m_ref, acc_ref,
                *, tq, tk, n_slc, n_top):
    i = pl.program_id(2)
    s0 = i * tq
    hd = B_HEAD_DIM
    rows = B_HPG * tq
    half = hd // ROPE_FRACTION // 2

    sub = pl.multiple_of((i % (tk // tq)) * tq, tq)
    qtabs = tuple(r[pl.ds(sub, tq), :] for r in (c_ref, s1_ref, s2_ref))
    qh = [(_rope(q_ref[:, h * hd:(h + 1) * hd].astype(F32), qtabs, half) * B_SCALE).astype(BF16)
          for h in range(B_HPG)]
    qs = jnp.concatenate(qh, axis=0)

    @pl.when(i % (tk // tq) == 0)
    def _():
        ktabs = (c_ref[...], s1_ref[...], s2_ref[...])
        k0 = pl.multiple_of(s0, tk)
        ksr_ref[pl.ds(k0, tk), :] = _rope(ks_ref[...].astype(F32), ktabs, half).astype(BF16)
        kwr_ref[pl.ds(k0, tk), :] = _rope(kw_ref[...].astype(F32), ktabs, half).astype(BF16)

    rid = lax.broadcasted_iota(jnp.int32, (rows, 1), 0)
    qpos_r = s0 + (rid & (tq - 1))

    n_cmp_pad = kc_ref.shape[0]
    cend = lax.broadcasted_iota(jnp.int32, (1, n_cmp_pad), 1) * CMP_STRIDE + (CMP_LEN - 1)
    sc = jnp.where(cend <= qpos_r, _dot_nt(qs, kc_ref[...]), NEG)
    pc = jnp.exp2(sc - jnp.max(sc, axis=-1, keepdims=True)).astype(BF16)
    pv = _dot(pc, _with_ones(vc_ref[...]))
    o_cmp = pv[:, 0:LANE] * jnp.where(qpos_r >= CMP_LEN - 1, 1.0 / pv[:, LANE:2 * LANE], 0.0)
    qpos_t = s0 + lax.broadcasted_iota(jnp.int32, (1, tq), 1)
    imp_t = jnp.zeros((LANE, tq), F32)
    for h in range(B_HPG):
        r = _dot_nt(ovt_ref[...], pc[h * tq:(h + 1) * tq])
        imp_t = imp_t + r[0:LANE] * jnp.where(qpos_t >= CMP_LEN - 1, 1.0 / r[LANE:LANE + 1], 0.0)

    wlen = WINDOW + tq
    w0 = pl.multiple_of(jnp.maximum(s0 - WINDOW, 0), tq)
    s_win = _dot_nt(qs, kwr_ref[pl.ds(w0, wlen), :])

    selb16 =_select_blocks(imp_t, s0, n_slc, n_top).T.astype(BF16)
    for h in range(B_HPG):
        qa_ref[h * tq:(h + 1) * tq, 0:hd] = qh[h]
        qa_ref[h * tq:(h + 1) * tq, hd:2 * hd] = selb16
    _flash_init(m_ref, acc_ref)

    def slc_scores(j, s_ref):
        k0 = pl.multiple_of(j * tk, tk)
        ka = jnp.concatenate([ksr_ref[pl.ds(k0, tk), :], e_ref[pl.ds(k0, tk), :]], axis=1)
        s_ref[...] = _dot_nt(qa_ref[...], ka)

    def slc_update(j, s_ref, causal=False):
        k0 = pl.multiple_of(j * tk, tk)
        s = s_ref[...]
        if causal:
            kpos = k0 + lax.broadcasted_iota(jnp.int32, (1, tk), 1)
            s = jnp.where(kpos <= qpos_r, s, NEG)
        _flash_update(s, vs_ref[pl.ds(k0, tk), :], m_ref, acc_ref)

    slc_scores(0, sa_ref)

    wpos = w0 + lax.broadcasted_iota(jnp.int32, (1, wlen), 1)
    wmask = (wpos <= qpos_r) & (wpos > qpos_r - WINDOW)
    o_win = _masked_attend(s_win, wmask, vw_ref[pl.ds(w0, wlen), :])

    def pair(jj, carry):
        j = 2 * jj
        slc_scores(j + 1, sb_ref)
        slc_update(j, sa_ref)
        slc_scores(j + 2, sa_ref)
        slc_update(j + 1, sb_ref)
        return carry

    jd = s0 // tk
    lax.fori_loop(0, jd // 2, pair, 0)

    @pl.when(jd % 2 == 1)
    def _():
        slc_scores(jd, sb_ref)
        slc_update(jd - 1, sa_ref)
        slc_update(jd, sb_ref, causal=True)

    @pl.when(jd % 2 == 0)
    def _():
        slc_update(jd, sa_ref, causal=True)

    o_slc = _flash_result(acc_ref)

    gates = jax.nn.sigmoid(gate_ref[...].astype(F32))
    for h in range(B_HPG):
        r = slice(h * tq, (h + 1) * tq)
        o = (gates[:, 3 * h:3 * h + 1] * o_cmp[r] + gates[:, 3 * h + 1:3 * h + 2] * o_slc[r]
             + gates[:, 3 * h + 2:3 * h + 3] * o_win[r])
        z = z_ref[:, h * hd:(h + 1) * hd].astype(F32)
        o_ref[:, h * hd:(h + 1) * hd] = (o * _silu(z)).astype(o_ref.dtype)


def _nsa_call(p3, kc, vc, e_mat, ovt_mat, tabs):
    b, s, _ = p3.shape
    tq, tk = 512, 512
    n_slc = s // SLC_BLOCK
    n_top = min(SLC_TOPK, n_slc)
    gw = B_HPG * B_HEAD_DIM
    rows = B_HPG * tq
    n_cmp_pad = kc.shape[2]
    per_k = tk // tq

    def slab(cb):
        return pl.BlockSpec((None, s, LANE), lambda bi, g, i: (bi, 0, cb + g))

    def ktile(cb):
        return pl.BlockSpec((None, tk, LANE), lambda bi, g, i: (bi, i // per_k, cb + g))

    cmp_spec = pl.BlockSpec((None, None, n_cmp_pad, B_HEAD_DIM), lambda bi, g, i: (bi, g, 0, 0))
    tab_spec = pl.BlockSpec((tk, LANE), lambda bi, g, i: (i // per_k, 0))
    return pl.pallas_call(
        functools.partial(_nsa_kernel, tq=tq, tk=tk, n_slc=n_slc, n_top=n_top),
        grid=(b, B_GROUPS, s // tq),
        in_specs=[pl.BlockSpec((None, tq, gw), lambda bi, g, i: (bi, i, CB_BQ * LANE // gw + g)),
                  cmp_spec, cmp_spec,
                  ktile(CB_BKS), slab(CB_BVS), ktile(CB_BKW), slab(CB_BVW),
                  pl.BlockSpec((s, LANE), lambda bi, g, i: (0, 0)),
                  pl.BlockSpec((LANE + SUBLANE, n_cmp_pad), lambda bi, g, i: (0, 0)),
                  pl.BlockSpec((None, tq, LANE), lambda bi, g, i: (bi, i, CB_BGATE + g)),
                  pl.BlockSpec((None, tq, gw), lambda bi, g, i: (bi, i, CB_BZ * LANE // gw + g)),
                  tab_spec, tab_spec, tab_spec],
        out_specs=pl.BlockSpec((None, tq, gw), lambda bi, g, i: (bi, i, g)),
        out_shape=jax.ShapeDtypeStruct((b, s, B_HEADS * B_HEAD_DIM), BF16),
        scratch_shapes=[pltpu.VMEM((rows, 2 * B_HEAD_DIM), BF16),
                        pltpu.VMEM((s, LANE), BF16),
                        pltpu.VMEM((s, LANE), BF16),
                        pltpu.VMEM((rows, tk), F32),
                        pltpu.VMEM((rows, tk), F32),
                        pltpu.VMEM((rows, LANE), F32),
                        pltpu.VMEM((rows, 2 * LANE), F32)],
        compiler_params=pltpu.CompilerParams(
            dimension_semantics=("arbitrary", "arbitrary", "arbitrary"), vmem_limit_bytes=VMEM_LIMIT),
        name="nsa",
    )(p3, kc, vc, p3, p3, p3, p3, e_mat, ovt_mat, p3, p3, *tabs)


def _out_kernel(oa_ref, ob_ref, ga_ref, gb_ref, x_ref, gate_ref, wa_ref, wb_ref, wo_ref, fg_ref, o_ref):
    ya = _dot(oa_ref[...], wa_ref[...])
    yb = _dot(ob_ref[...], wb_ref[...])
    mix = (jax.nn.sigmoid(ga_ref[...].astype(F32)) * ya
           + jax.nn.sigmoid(gb_ref[...].astype(F32)) * yb)
    y = _dot(mix.astype(BF16), wo_ref[...])
    xo = x_ref[...] + gate_ref[...] * y
    o_ref[...] = xo * lax.rsqrt(jnp.mean(xo * xo, axis=-1, keepdims=True) + EPS) * fg_ref[...]


def _out_call(oa, ob, p2, x2, gate, wa, wb, wo, fg, seq):
    t, d = x2.shape
    tm = 256
    per_seq = seq // tm
    aw = oa.shape[1]
    bw = ob.shape[1]

    def resident(a):
        return pl.BlockSpec(a.shape, lambda i: (0, 0), pipeline_mode=pl.Buffered(1))

    return pl.pallas_call(
        _out_kernel,
        grid=(t // tm,),
        in_specs=[pl.BlockSpec((tm, aw), lambda i: (i, 0)),
                  pl.BlockSpec((tm, bw), lambda i: (i, 0)),
                  pl.BlockSpec((tm, d), lambda i: (i, CB_GA * LANE // d)),
                  pl.BlockSpec((tm, d), lambda i: (i, CB_GB * LANE // d)),
                  pl.BlockSpec((tm, d), lambda i: (i, 0)),
                  pl.BlockSpec((None, 1, d), lambda i: (i // per_seq, 0, 0)),
                  resident(wa), resident(wb), resident(wo),
                  pl.BlockSpec((1, d), lambda i: (0, 0))],
        out_specs=pl.BlockSpec((tm, d), lambda i: (i, 0)),
        out_shape=jax.ShapeDtypeStruct((t, d), F32),
        compiler_params=pltpu.CompilerParams(
            dimension_semantics=("arbitrary",), vmem_limit_bytes=VMEM_LIMIT),
        name="out_proj",
    )(oa, ob, p2, p2, x2, gate, wa, wb, wo, fg)


def _rope_tables(pos, head_dim):
    rd = head_dim // ROPE_FRACTION
    half = rd // 2
    inv = 1.0 / (ROPE_THETA ** (jnp.arange(half, dtype=F32) * (2.0 / rd)))
    ang = pos.astype(F32)[:, None] * inv[None, :]
    cos, sin = jnp.cos(ang), jnp.sin(ang)
    n = pos.shape[0]
    pad = jnp.zeros((n, head_dim - rd), F32)
    zero = jnp.zeros((n, half), F32)
    c = jnp.concatenate([cos, cos, pad + 1.0], axis=1)
    s1 = jnp.concatenate([-sin, zero, pad], axis=1)
    s2 = jnp.concatenate([zero, sin, pad], axis=1)
    reps = LANE // head_dim
    return tuple(jnp.tile(a, (1, reps)) for a in (c, s1, s2))


def _gate_weights(w):
    d = w.shape[0]
    gpg = B_HPG * 3
    a_g = W_MAIN_COLS
    a_m = a_g + B_GROUPS * gpg
    zpad = jnp.zeros((d, LANE - gpg), w.dtype)
    return jnp.concatenate([w[:, a_m:a_m + 2 * D_MODEL],
                            w[:, a_g:a_g + gpg], zpad, w[:, a_g + gpg:a_g + 2 * gpg], zpad,
                            jnp.zeros((d, 2 * LANE), w.dtype)], axis=1)


def kernel(x, c, w_ada, b_ada, norm_g, w_in, lambda_q1, lambda_k1, lambda_q2, lambda_k2, diff_norm_g,
           cmp_pe_k, cmp_pe_v, cmp_w1_k, cmp_w2_k, cmp_w1_v, cmp_w2_v, w_branch, w_out, final_norm_g):
    b, s, d = x.shape
    assert d == D_MODEL and s % 1024 == 0 and s // SLC_BLOCK <= LANE
    t = b * s
    x2 = x.reshape(t, d)

    c_pad = jnp.pad(c, ((0, 8 - b % 8 if b % 8 else 0), (0, 0)))
    mod = _ada_call(c_pad, w_ada[0], b_ada[0][None, :])[:b]
    shift = mod[:, None, 0:d]
    scale = mod[:, None, d:2 * d]
    gate = mod[:, None, 2 * d:3 * d]

    w_bf = w_in[0].astype(BF16)
    p2 = _inproj_call(x2, norm_g[0][None, :], scale, shift, _gate_weights(w_bf), w_bf, s)
    p3 = p2.reshape(b, s, P_WIDTH)

    pos = jnp.arange(s)
    oa = _diff_call(p3, _rope_tables(pos, A_QK_DIM), lambda_q1, lambda_k1, lambda_q2, lambda_k2, diff_norm_g)

    n_str = s // CMP_STRIDE
    hw = CMP_STRIDE * B_HEAD_DIM

    def strides(cb):
        return p3[:, :, cb * LANE:(cb + B_GROUPS) * LANE].reshape(b, n_str, CMP_STRIDE * B_GROUPS * B_HEAD_DIM)

    cmp_end = jnp.arange(n_str) * CMP_STRIDE + (CMP_LEN - 1)
    kc, vc = _compress_call(
        strides(CB_BKC), strides(CB_BVC),
        cmp_pe_k[0].reshape(2, hw), cmp_pe_v[0].reshape(2, hw),
        cmp_w1_k[0].astype(BF16), cmp_w2_k[0].astype(BF16),
        cmp_w1_v[0].astype(BF16), cmp_w2_v[0].astype(BF16),
        _rope_tables(cmp_end, B_HEAD_DIM))
    e_mat = (jnp.arange(s)[:, None] // SLC_BLOCK == jnp.arange(LANE)[None, :]).astype(BF16)
    cmp_start = jnp.arange(n_str) * CMP_STRIDE
    slc_start = jnp.arange(LANE) * SLC_BLOCK
    ovt_mat = ((cmp_start[None, :] < slc_start[:, None] + SLC_BLOCK)
               & (cmp_start[None, :] + CMP_LEN > slc_start[:, None])).astype(BF16)
    ovt_mat = jnp.concatenate([ovt_mat, jnp.ones((SUBLANE, n_str), BF16)], axis=0)
    ob = _nsa_call(p3, kc, vc, e_mat, ovt_mat, _rope_tables(pos, B_HEAD_DIM))

    wbr = w_branch[0].astype(BF16)
    a_w = A_HEADS * A_V_DIM
    out = _out_call(oa.reshape(t, a_w), ob.reshape(t, B_HEADS * B_HEAD_DIM), p2, x2, gate,
                    wbr[:a_w], wbr[a_w:], w_out[0].astype(BF16), final_norm_g[None, :], s)
    return out.reshape(b, s, d)
```

```python
import functools
import math

import jax
import jax.numpy as jnp
from jax import lax
from jax.experimental import pallas as pl
from jax.experimental.pallas import tpu as pltpu

F32 = jnp.float32
BF16 = jnp.bfloat16

D_MODEL = 2048
A_HEADS = 8
A_QK_DIM = 64
A_V_DIM = 128
B_HEADS = 8
B_GROUPS = 2
B_HPG = B_HEADS // B_GROUPS
B_HEAD_DIM = 128
CMP_LEN = 32
CMP_STRIDE = 16
CMP_HIDDEN = 256
SLC_BLOCK = 64
SLC_TOPK = 16
N_FORCED = 3
WINDOW = 512
FORCE_SCORE = 1.0e4
ROPE_THETA = 500000.0
ROPE_FRACTION = 4
EPS = 1e-6
NEG = -1e30
TAKEN = -3.0e38
LAM_INIT = 0.8 - 0.6 * math.exp(-0.3 * 0)
LOG2E = math.log2(math.e)

LANE = 128
SUBLANE = 8
VMEM_LIMIT = 56 * 1024 * 1024

CB_GA, CB_GB, CB_BGATE, CB_MAIN = 0, 16, 32, 36
CB_AQ, CB_AK, CB_AV, CB_AZ, CB_BQ = (CB_MAIN + o for o in (0, 8, 16, 24, 32))
CB_BKC, CB_BVC, CB_BKS, CB_BVS, CB_BKW, CB_BVW, CB_BZ = (CB_MAIN + o for o in (40, 42, 44, 46, 48, 50, 52))
P_WIDTH = 96 * LANE
PROJ_TN = 1536
N_TAIL_TILES = (CB_MAIN * LANE) // PROJ_TN
W_MAIN_COLS = (CB_BZ + 8 - CB_MAIN) * LANE
A_SCALE = LOG2E * A_QK_DIM ** -0.5
B_SCALE = LOG2E * B_HEAD_DIM ** -0.5


def _dot(a, b):
    return jnp.dot(a, b, preferred_element_type=F32)


def _dot_nt(a, b):
    return lax.dot_general(a, b, (((1,), (1,)), ((), ())), preferred_element_type=F32)


def _silu(v):
    return v * jax.nn.sigmoid(v)


def _rope(x, tabs, half):
    c, s1, s2 = tabs
    return x * c + pltpu.roll(x, LANE - half, axis=1) * s1 + pltpu.roll(x, half, axis=1) * s2


def _ada_kernel(c_ref, w_ref, b_ref, o_ref):
    o_ref[...] = _dot(_silu(c_ref[...]), w_ref[...]) + b_ref[...]


def _ada_call(c_pad, w, b):
    rows, d = c_pad.shape
    n = w.shape[1]
    tn = 768
    return pl.pallas_call(
        _ada_kernel,
        grid=(n // tn,),
        in_specs=[pl.BlockSpec((rows, d), lambda j: (0, 0)),
                  pl.BlockSpec((d, tn), lambda j: (0, j)),
                  pl.BlockSpec((1, tn), lambda j: (0, j))],
        out_specs=pl.BlockSpec((rows, tn), lambda j: (0, j)),
        out_shape=jax.ShapeDtypeStruct((rows, n), F32),
        compiler_params=pltpu.CompilerParams(vmem_limit_bytes=VMEM_LIMIT),
        name="ada",
    )(c_pad, w, b)


def _inproj_kernel(x_ref, g_ref, sc_ref, sh_ref, wt_ref, wm_ref, o_ref, h_ref):
    j = pl.program_id(1)

    @pl.when(j == 0)
    def _():
        x = x_ref[...]
        ms = jnp.mean(x * x, axis=-1, keepdims=True)
        xn = x * lax.rsqrt(ms + EPS)
        h_ref[...] = (xn * (g_ref[...] * (1.0 + sc_ref[...])) + sh_ref[...]).astype(h_ref.dtype)

    @pl.when(j < N_TAIL_TILES)
    def _():
        o_ref[...] = _dot(h_ref[...], wt_ref[...]).astype(o_ref.dtype)

    @pl.when(j >= N_TAIL_TILES)
    def _():
        o_ref[...] = _dot(h_ref[...], wm_ref[...]).astype(o_ref.dtype)


def _inproj_call(x2, g, scale, shift, w_tail, w_main, seq):
    t, d = x2.shape
    tm = min(1024, seq)
    per_seq = seq // tm
    mod = pl.BlockSpec((None, 1, d), lambda i, j: (i // per_seq, 0, 0))
    return pl.pallas_call(
        _inproj_kernel,
        grid=(t // tm, P_WIDTH // PROJ_TN),
        in_specs=[pl.BlockSpec((tm, d), lambda i, j: (i, 0)),
                  pl.BlockSpec((1, d), lambda i, j: (0, 0)),
                  mod, mod,
                  pl.BlockSpec((d, PROJ_TN), lambda i, j: (0, jnp.minimum(j, N_TAIL_TILES - 1))),
                  pl.BlockSpec((d, PROJ_TN), lambda i, j: (0, jnp.maximum(j - N_TAIL_TILES, 0)))],
        out_specs=pl.BlockSpec((tm, PROJ_TN), lambda i, j: (i, j)),
        out_shape=jax.ShapeDtypeStruct((t, P_WIDTH), BF16),
        scratch_shapes=[pltpu.VMEM((tm, d), BF16)],
        compiler_params=pltpu.CompilerParams(
            dimension_semantics=("arbitrary", "arbitrary"), vmem_limit_bytes=VMEM_LIMIT),
        name="inproj",
    )(x2, g, scale, shift, w_tail, w_main)


def _flash_init(m_ref, acc_ref):
    m_ref[...] = jnp.full(m_ref.shape, NEG, F32)
    acc_ref[...] = jnp.zeros(acc_ref.shape, F32)


def _with_ones(v):
    return jnp.concatenate([v, jnp.ones(v.shape, v.dtype)], axis=1)


def _flash_update(s, v, m_ref, acc_ref, first=False):
    m_cur = jnp.max(s, axis=-1, keepdims=True)
    if first:
        m_new = jnp.broadcast_to(m_cur, m_ref.shape)
    else:
        m_old = m_ref[...]
        m_new = jnp.maximum(m_old, m_cur)
        alpha = jnp.exp2(m_old - m_new)
    p = jnp.exp2(s - jnp.tile(m_new, (1, s.shape[1] // LANE)))
    pv = _dot(p.astype(v.dtype), _with_ones(v))
    acc_ref[...] = pv if first else jnp.tile(alpha, (1, 2)) * acc_ref[...] + pv
    m_ref[...] = m_new


def _flash_result(acc_ref):
    acc = acc_ref[...]
    return acc[:, 0:LANE] / acc[:, LANE:2 * LANE]


def _masked_attend(s, mask, v):
    s = jnp.where(mask, s, NEG)
    p = jnp.exp2(s - jnp.max(s, axis=-1, keepdims=True))
    pv = _dot(p.astype(v.dtype), _with_ones(v))
    return pv[:, 0:LANE] / pv[:, LANE:2 * LANE]


def _run_tiles(first, last, scores, update, bufs, before_last=None):
    n = last - first + 1
    nq = (n - 1) // 4

    def quad(qi, carry):
        j = first + 4 * qi
        for k in range(4):
            scores(j + k + 1, bufs[(k + 1) % 2])
            update(j + k, bufs[k % 2])
        return carry

    lax.fori_loop(0, nq, quad, 0)
    base = first + 4 * nq
    rem = n - 4 * nq

    def tail(r):
        for k in range(r - 1):
            scores(base + k + 1, bufs[(k + 1) % 2])
            update(base + k, bufs[k % 2])
        if before_last is not None:
            before_last()
        update(base + r - 1, bufs[(r - 1) % 2], True)

    for r in range(1, 5):
        pl.when(rem == r)(functools.partial(tail, r))


def _diff_kernel(q_ref, qn_ref, k_ref, v_ref, z_ref, c_ref, s1_ref, s2_ref, cn_ref, s1n_ref, s2n_ref,
                 lq1_ref, lk1_ref, lq2_ref, lk2_ref, g_ref, o_ref,
                 qs_ref, qsn_ref, kr_ref, sa_ref, sb_ref, sc_ref, m_ref, acc_ref, *, tq):
    i = pl.program_id(2)
    half = A_QK_DIM // ROPE_FRACTION // 2

    def build_qs(src_ref, tab_refs, dst_ref):
        q = _rope(src_ref[...].astype(F32), tuple(r[...] for r in tab_refs), half) * A_SCALE
        lane = lax.broadcasted_iota(jnp.int32, q.shape, 1)
        dst_ref[0:tq, :] = jnp.where(lane < A_QK_DIM, q, 0.0).astype(dst_ref.dtype)
        dst_ref[tq:2 * tq, :] = jnp.where(lane >= A_QK_DIM, q, 0.0).astype(dst_ref.dtype)

    def scores(j, s_ref):
        s_ref[...] = _dot_nt(qs_ref[...], kr_ref[pl.ds(pl.multiple_of(j * tq, tq), tq), :])

    def update(j, s_ref, causal=False, first=False):
        s = s_ref[...]
        if causal:
            row = lax.broadcasted_iota(jnp.int32, s.shape, 0)
            col = lax.broadcasted_iota(jnp.int32, s.shape, 1)
            s = jnp.where(col <= (row & (tq - 1)), s, NEG)
        _flash_update(s, v_ref[pl.ds(pl.multiple_of(j * tq, tq), tq), :], m_ref, acc_ref, first)

    def build_next():
        build_qs(qn_ref, (cn_ref, s1n_ref, s2n_ref), qsn_ref)

    def prefetch():
        sc_ref[...] = _dot_nt(qsn_ref[...], kr_ref[0:tq, :])

    def rope_keys():
        ktabs = (c_ref[...], s1_ref[...], s2_ref[...])
        kr_ref[pl.ds(pl.multiple_of(i * tq, tq), tq), :] = (
            _rope(k_ref[...].astype(F32), ktabs, half).astype(kr_ref.dtype))

    @pl.when(i == 0)
    def _():
        rope_keys()
        build_qs(q_ref, (c_ref, s1_ref, s2_ref), qs_ref)
        sc_ref[...] = _dot_nt(qs_ref[...], kr_ref[0:tq, :])
        build_next()
        update(0, sc_ref, True, True)
        prefetch()

    @pl.when(i > 0)
    def _():
        qs_ref[...] = qsn_ref[...]
        rope_keys()
        scores(1, sa_ref)
        build_next()
        update(0, sc_ref, False, True)
        _run_tiles(1, i, scores, update, (sa_ref, sb_ref), before_last=prefetch)

    o = _flash_result(acc_ref)
    lam = (jnp.exp(jnp.sum(lq1_ref[...] * lk1_ref[...], axis=-1, keepdims=True))
           - jnp.exp(jnp.sum(lq2_ref[...] * lk2_ref[...], axis=-1, keepdims=True)) + LAM_INIT)
    d = o[0:tq] - lam * o[tq:2 * tq]
    dn = d * lax.rsqrt(jnp.mean(d * d, axis=-1, keepdims=True) + EPS)
    dn = dn * g_ref[...] * (1.0 - LAM_INIT)
    o_ref[...] = (dn * _silu(z_ref[...].astype(F32))).astype(o_ref.dtype)


def _diff_call(p3, tabs, lq1, lk1, lq2, lk2, sub_g):
    b, s, _ = p3.shape
    tq = 512
    last = s // tq - 1
    lam_spec = pl.BlockSpec((1, A_QK_DIM), lambda bi, h, i: (0, 0))
    tab_spec = pl.BlockSpec((tq, LANE), lambda bi, h, i: (i, 0))
    tabn_spec = pl.BlockSpec((tq, LANE), lambda bi, h, i: (jnp.minimum(i + 1, last), 0))
    return pl.pallas_call(
        functools.partial(_diff_kernel, tq=tq),
        grid=(b, A_HEADS, s // tq),
        in_specs=[pl.BlockSpec((None, tq, LANE), lambda bi, h, i: (bi, i, CB_AQ + h)),
                  pl.BlockSpec((None, tq, LANE), lambda bi, h, i: (bi, jnp.minimum(i + 1, last), CB_AQ + h)),
                  pl.BlockSpec((None, tq, LANE), lambda bi, h, i: (bi, i, CB_AK + h)),
                  pl.BlockSpec((None, s, LANE), lambda bi, h, i: (bi, 0, CB_AV + h)),
                  pl.BlockSpec((None, tq, LANE), lambda bi, h, i: (bi, i, CB_AZ + h)),
                  tab_spec, tab_spec, tab_spec, tabn_spec, tabn_spec, tabn_spec,
                  lam_spec, lam_spec, lam_spec, lam_spec,
                  pl.BlockSpec((1, A_V_DIM), lambda bi, h, i: (0, 0))],
        out_specs=pl.BlockSpec((None, tq, LANE), lambda bi, h, i: (bi, i, h)),
        out_shape=jax.ShapeDtypeStruct((b, s, A_HEADS * A_V_DIM), BF16),
        scratch_shapes=[pltpu.VMEM((2 * tq, LANE), BF16),
                        pltpu.VMEM((2 * tq, LANE), BF16),
                        pltpu.VMEM((s, LANE), BF16),
                        pltpu.VMEM((2 * tq, tq), F32),
                        pltpu.VMEM((2 * tq, tq), F32),
                        pltpu.VMEM((2 * tq, tq), F32),
                        pltpu.VMEM((2 * tq, LANE), F32),
                        pltpu.VMEM((2 * tq, 2 * LANE), F32)],
        compiler_params=pltpu.CompilerParams(
            dimension_semantics=("arbitrary", "arbitrary", "arbitrary"), vmem_limit_bytes=VMEM_LIMIT),
        name="diff_attn",
    )(p3, p3, p3, p3, p3, *tabs, *tabs, lq1, lk1, lq2, lk2, sub_g)


def _compress_kernel(xk_ref, xv_ref, pek_ref, pev_ref, w1k_ref, w2k_ref, w1v_ref, w2v_ref,
                     c_ref, s1_ref, s2_ref, kc_ref, vc_ref):
    n = xk_ref.shape[0]
    hd = B_HEAD_DIM
    half_w = CMP_STRIDE * hd
    tabs = (c_ref[...], s1_ref[...], s2_ref[...])

    def mlp(x_ref, g, pe_ref, w1_ref, w2_ref):
        x = jnp.concatenate([x_ref[:, (r * B_GROUPS + g) * hd:(r * B_GROUPS + g + 1) * hd]
                             for r in range(CMP_STRIDE)], axis=1).astype(F32)
        top = _dot((x + pe_ref[0:1, :]).astype(BF16), w1_ref[0:half_w, :])
        bot = _dot((x + pe_ref[1:2, :]).astype(BF16), w1_ref[half_w:2 * half_w, :])
        hid = top + pltpu.roll(bot, n - 1, axis=0)
        return _dot(_silu(hid).astype(BF16), w2_ref[...])

    for g in range(B_GROUPS):
        kc = mlp(xk_ref, g, pek_ref, w1k_ref, w2k_ref)
        kc_ref[g] = _rope(kc, tabs, hd // ROPE_FRACTION // 2).astype(kc_ref.dtype)
        vc_ref[g] = mlp(xv_ref, g, pev_ref, w1v_ref, w2v_ref).astype(vc_ref.dtype)


def _compress_call(xk, xv, pek, pev, w1k, w2k, w1v, w2v, tabs):
    b, n, xw = xk.shape
    x_spec = pl.BlockSpec((None, n, xw), lambda bi: (bi, 0, 0))
    o_spec = pl.BlockSpec((None, B_GROUPS, n, B_HEAD_DIM), lambda bi: (bi, 0, 0, 0))

    def full(a):
        return pl.BlockSpec(a.shape, lambda bi: (0,) * a.ndim)

    consts = (pek, pev, w1k, w2k, w1v, w2v, *tabs)
    return pl.pallas_call(
        _compress_kernel,
        grid=(b,),
        in_specs=[x_spec, x_spec] + [full(a) for a in consts],
        out_specs=[o_spec, o_spec],
        out_shape=[jax.ShapeDtypeStruct((b, B_GROUPS, n, B_HEAD_DIM), BF16)] * 2,
        compiler_params=pltpu.CompilerParams(
            dimension_semantics=("arbitrary",), vmem_limit_bytes=VMEM_LIMIT),
        name="compress",
    )(xk, xv, *consts)


def _select_blocks(imp_t, s0, n_slc, n_top):
    shape = imp_t.shape
    jb = lax.broadcasted_iota(jnp.int32, shape, 0)
    jbf = jb.astype(F32)
    qpos = s0 + lax.broadcasted_iota(jnp.int32, shape, 1)
    cur = qpos // SLC_BLOCK
    valid = (jb * SLC_BLOCK <= qpos) & (jb < n_slc)
    forced = (jb == 0) | (jb == cur) | (jb == cur - 1)
    imp = jnp.where(valid, jnp.where(forced, TAKEN, imp_t), NEG)
    for _ in range(n_top - N_FORCED):
        mx = jnp.max(imp, axis=0, keepdims=True)
        idx = jnp.min(jnp.where(imp == mx, jbf, float(LANE)), axis=0, keepdims=True)
        imp = jnp.where(jbf == idx, TAKEN, imp)
    return jnp.where((imp == TAKEN) & valid, 0.0, NEG)


def _nsa_kernel(q_ref, kc_ref, vc_ref, ks_ref, vs_ref, kw_ref, vw_ref, e_ref, ovt_ref, gate_ref, z_ref,
                c_ref, s1_ref, s2_ref, o_ref, qa_ref, ksr_ref, kwr_ref, sa_ref, sb_ref, m_ref, acc_ref,
                *, tq, tk, n_slc, n_top):
    i = pl.program_id(2)
    s0 = i * tq
    hd = B_HEAD_DIM
    rows = B_HPG * tq
    half = hd // ROPE_FRACTION // 2

    sub = pl.multiple_of((i % (tk // tq)) * tq, tq)
    qtabs = tuple(r[pl.ds(sub, tq), :] for r in (c_ref, s1_ref, s2_ref))
    qh = [(_rope(q_ref[:, h * hd:(h + 1) * hd].astype(F32), qtabs, half) * B_SCALE).astype(BF16)
          for h in range(B_HPG)]
    qs = jnp.concatenate(qh, axis=0)

    @pl.when(i % (tk // tq) == 0)
    def _():
        ktabs = (c_ref[...], s1_ref[...], s2_ref[...])
        k0 = pl.multiple_of(s0, tk)
        ksr_ref[pl.ds(k0, tk), :] = _rope(ks_ref[...].astype(F32), ktabs, half).astype(BF16)
        kwr_ref[pl.ds(k0, tk), :] = _rope(kw_ref[...].astype(F32), ktabs, half).astype(BF16)

    rid = lax.broadcasted_iota(jnp.int32, (rows, 1), 0)
    qpos_r = s0 + (rid & (tq - 1))

    n_cmp_pad = kc_ref.shape[0]
    cend = lax.broadcasted_iota(jnp.int32, (1, n_cmp_pad), 1) * CMP_STRIDE + (CMP_LEN - 1)
    sc = jnp.where(cend <= qpos_r, _dot_nt(qs, kc_ref[...]), NEG)
    pc = jnp.exp2(sc - jnp.max(sc, axis=-1, keepdims=True)).astype(BF16)
    pv = _dot(pc, _with_ones(vc_ref[...]))
    o_cmp = pv[:, 0:LANE] * jnp.where(qpos_r >= CMP_LEN - 1, 1.0 / pv[:, LANE:2 * LANE], 0.0)
    qpos_t = s0 + lax.broadcasted_iota(jnp.int32, (1, tq), 1)
    imp_t = jnp.zeros((LANE, tq), F32)
    for h in range(B_HPG):
        r = _dot_nt(ovt_ref[...], pc[h * tq:(h + 1) * tq])
        imp_t = imp_t + r[0:LANE] * jnp.where(qpos_t >= CMP_LEN - 1, 1.0 / r[LANE:LANE + 1], 0.0)

    wlen = WINDOW + tq
    w0 = pl.multiple_of(jnp.maximum(s0 - WINDOW, 0), tq)
    s_win = _dot_nt(qs, kwr_ref[pl.ds(w0, wlen), :])

    selb16 =_select_blocks(imp_t, s0, n_slc, n_top).T.astype(BF16)
    for h in range(B_HPG):
        qa_ref[h * tq:(h + 1) * tq, 0:hd] = qh[h]
        qa_ref[h * tq:(h + 1) * tq, hd:2 * hd] = selb16
    _flash_init(m_ref, acc_ref)

    def slc_scores(j, s_ref):
        k0 = pl.multiple_of(j * tk, tk)
        ka = jnp.concatenate([ksr_ref[pl.ds(k0, tk), :], e_ref[pl.ds(k0, tk), :]], axis=1)
        s_ref[...] = _dot_nt(qa_ref[...], ka)

    def slc_update(j, s_ref, causal=False):
        k0 = pl.multiple_of(j * tk, tk)
        s = s_ref[...]
        if causal:
            kpos = k0 + lax.broadcasted_iota(jnp.int32, (1, tk), 1)
            s = jnp.where(kpos <= qpos_r, s, NEG)
        _flash_update(s, vs_ref[pl.ds(k0, tk), :], m_ref, acc_ref)

    slc_scores(0, sa_ref)

    wpos = w0 + lax.broadcasted_iota(jnp.int32, (1, wlen), 1)
    wmask = (wpos <= qpos_r) & (wpos > qpos_r - WINDOW)
    o_win = _masked_attend(s_win, wmask, vw_ref[pl.ds(w0, wlen), :])

    _run_tiles(0, s0 // tk, slc_scores, slc_update, (sa_ref, sb_ref))
    o_slc = _flash_result(acc_ref)

    gates = jax.nn.sigmoid(gate_ref[...].astype(F32))
    for h in range(B_HPG):
        r = slice(h * tq, (h + 1) * tq)
        o = (gates[:, 3 * h:3 * h + 1] * o_cmp[r] + gates[:, 3 * h + 1:3 * h + 2] * o_slc[r]
             + gates[:, 3 * h + 2:3 * h + 3] * o_win[r])
        z = z_ref[:, h * hd:(h + 1) * hd].astype(F32)
        o_ref[:, h * hd:(h + 1) * hd] = (o * _silu(z)).astype(o_ref.dtype)


def _nsa_call(p3, kc, vc, e_mat, ovt_mat, tabs):
    b, s, _ = p3.shape
    tq, tk = 512, 512
    n_slc = s // SLC_BLOCK
    n_top = min(SLC_TOPK, n_slc)
    gw = B_HPG * B_HEAD_DIM
    rows = B_HPG * tq
    n_cmp_pad = kc.shape[2]
    per_k = tk // tq

    def slab(cb):
        return pl.BlockSpec((None, s, LANE), lambda bi, g, i: (bi, 0, cb + g))

    def ktile(cb):
        return pl.BlockSpec((None, tk, LANE), lambda bi, g, i: (bi, i // per_k, cb + g))

    cmp_spec = pl.BlockSpec((None, None, n_cmp_pad, B_HEAD_DIM), lambda bi, g, i: (bi, g, 0, 0))
    tab_spec = pl.BlockSpec((tk, LANE), lambda bi, g, i: (i // per_k, 0))
    return pl.pallas_call(
        functools.partial(_nsa_kernel, tq=tq, tk=tk, n_slc=n_slc, n_top=n_top),
        grid=(b, B_GROUPS, s // tq),
        in_specs=[pl.BlockSpec((None, tq, gw), lambda bi, g, i: (bi, i, CB_BQ * LANE // gw + g)),
                  cmp_spec, cmp_spec,
                  ktile(CB_BKS), slab(CB_BVS), ktile(CB_BKW), slab(CB_BVW),
                  pl.BlockSpec((s, LANE), lambda bi, g, i: (0, 0)),
                  pl.BlockSpec((LANE + SUBLANE, n_cmp_pad), lambda bi, g, i: (0, 0)),
                  pl.BlockSpec((None, tq, LANE), lambda bi, g, i: (bi, i, CB_BGATE + g)),
                  pl.BlockSpec((None, tq, gw), lambda bi, g, i: (bi, i, CB_BZ * LANE // gw + g)),
                  tab_spec, tab_spec, tab_spec],
        out_specs=pl.BlockSpec((None, tq, gw), lambda bi, g, i: (bi, i, g)),
        out_shape=jax.ShapeDtypeStruct((b, s, B_HEADS * B_HEAD_DIM), BF16),
        scratch_shapes=[pltpu.VMEM((rows, 2 * B_HEAD_DIM), BF16),
                        pltpu.VMEM((s, LANE), BF16),
                        pltpu.VMEM((s, LANE), BF16),
                        pltpu.VMEM((rows, tk), F32),
                        pltpu.VMEM((rows, tk), F32),
                        pltpu.VMEM((rows, LANE), F32),
                        pltpu.VMEM((rows, 2 * LANE), F32)],
        compiler_params=pltpu.CompilerParams(
            dimension_semantics=("arbitrary", "arbitrary", "arbitrary"), vmem_limit_bytes=VMEM_LIMIT),
        name="nsa",
    )(p3, kc, vc, p3, p3, p3, p3, e_mat, ovt_mat, p3, p3, *tabs)


def _out_kernel(oa_ref, ob_ref, ga_ref, gb_ref, x_ref, gate_ref, wa_ref, wb_ref, wo_ref, fg_ref, o_ref):
    ya = _dot(oa_ref[...], wa_ref[...])
    yb = _dot(ob_ref[...], wb_ref[...])
    mix = (jax.nn.sigmoid(ga_ref[...].astype(F32)) * ya
           + jax.nn.sigmoid(gb_ref[...].astype(F32)) * yb)
    y = _dot(mix.astype(BF16), wo_ref[...])
    xo = x_ref[...] + gate_ref[...] * y
    o_ref[...] = xo * lax.rsqrt(jnp.mean(xo * xo, axis=-1, keepdims=True) + EPS) * fg_ref[...]


def _out_call(oa, ob, p2, x2, gate, wa, wb, wo, fg, seq):
    t, d = x2.shape
    tm = 256
    per_seq = seq // tm
    aw = oa.shape[1]
    bw = ob.shape[1]

    def resident(a):
        return pl.BlockSpec(a.shape, lambda i: (0, 0), pipeline_mode=pl.Buffered(1))

    return pl.pallas_call(
        _out_kernel,
        grid=(t // tm,),
        in_specs=[pl.BlockSpec((tm, aw), lambda i: (i, 0)),
                  pl.BlockSpec((tm, bw), lambda i: (i, 0)),
                  pl.BlockSpec((tm, d), lambda i: (i, CB_GA * LANE // d)),
                  pl.BlockSpec((tm, d), lambda i: (i, CB_GB * LANE // d)),
                  pl.BlockSpec((tm, d), lambda i: (i, 0)),
                  pl.BlockSpec((None, 1, d), lambda i: (i // per_seq, 0, 0)),
                  resident(wa), resident(wb), resident(wo),
                  pl.BlockSpec((1, d), lambda i: (0, 0))],
        out_specs=pl.BlockSpec((tm, d), lambda i: (i, 0)),
        out_shape=jax.ShapeDtypeStruct((t, d), F32),
        compiler_params=pltpu.CompilerParams(
            dimension_semantics=("arbitrary",), vmem_limit_bytes=VMEM_LIMIT),
        name="out_proj",
    )(oa, ob, p2, p2, x2, gate, wa, wb, wo, fg)


def _rope_tables(pos, head_dim):
    rd = head_dim // ROPE_FRACTION
    half = rd // 2
    inv = 1.0 / (ROPE_THETA ** (jnp.arange(half, dtype=F32) * (2.0 / rd)))
    ang = pos.astype(F32)[:, None] * inv[None, :]
    cos, sin = jnp.cos(ang), jnp.sin(ang)
    n = pos.shape[0]
    pad = jnp.zeros((n, head_dim - rd), F32)
    zero = jnp.zeros((n, half), F32)
    c = jnp.concatenate([cos, cos, pad + 1.0], axis=1)
    s1 = jnp.concatenate([-sin, zero, pad], axis=1)
    s2 = jnp.concatenate([zero, sin, pad], axis=1)
    reps = LANE // head_dim
    return tuple(jnp.tile(a, (1, reps)) for a in (c, s1, s2))


def _gate_weights(w):
    d = w.shape[0]
    gpg = B_HPG * 3
    a_g = W_MAIN_COLS
    a_m = a_g + B_GROUPS * gpg
    zpad = jnp.zeros((d, LANE - gpg), w.dtype)
    return jnp.concatenate([w[:, a_m:a_m + 2 * D_MODEL],
                            w[:, a_g:a_g + gpg], zpad, w[:, a_g + gpg:a_g + 2 * gpg], zpad,
                            jnp.zeros((d, 2 * LANE), w.dtype)], axis=1)


def kernel(x, c, w_ada, b_ada, norm_g, w_in, lambda_q1, lambda_k1, lambda_q2, lambda_k2, diff_norm_g,
           cmp_pe_k, cmp_pe_v, cmp_w1_k, cmp_w2_k, cmp_w1_v, cmp_w2_v, w_branch, w_out, final_norm_g):
    b, s, d = x.shape
    assert d == D_MODEL and s % 1024 == 0 and s // SLC_BLOCK <= LANE
    t = b * s
    x2 = x.reshape(t, d)

    c_pad = jnp.pad(c, ((0, 8 - b % 8 if b % 8 else 0), (0, 0)))
    mod = _ada_call(c_pad, w_ada[0], b_ada[0][None, :])[:b]
    shift = mod[:, None, 0:d]
    scale = mod[:, None, d:2 * d]
    gate = mod[:, None, 2 * d:3 * d]

    w_bf = w_in[0].astype(BF16)
    p2 = _inproj_call(x2, norm_g[0][None, :], scale, shift, _gate_weights(w_bf), w_bf, s)
    p3 = p2.reshape(b, s, P_WIDTH)

    pos = jnp.arange(s)
    oa = _diff_call(p3, _rope_tables(pos, A_QK_DIM), lambda_q1, lambda_k1, lambda_q2, lambda_k2, diff_norm_g)

    n_str = s // CMP_STRIDE
    hw = CMP_STRIDE * B_HEAD_DIM

    def strides(cb):
        return p3[:, :, cb * LANE:(cb + B_GROUPS) * LANE].reshape(b, n_str, CMP_STRIDE * B_GROUPS * B_HEAD_DIM)

    cmp_end = jnp.arange(n_str) * CMP_STRIDE + (CMP_LEN - 1)
    kc, vc = _compress_call(
        strides(CB_BKC), strides(CB_BVC),
        cmp_pe_k[0].reshape(2, hw), cmp_pe_v[0].reshape(2, hw),
        cmp_w1_k[0].astype(BF16), cmp_w2_k[0].astype(BF16),
        cmp_w1_v[0].astype(BF16), cmp_w2_v[0].astype(BF16),
        _rope_tables(cmp_end, B_HEAD_DIM))
    e_mat = (jnp.arange(s)[:, None] // SLC_BLOCK == jnp.arange(LANE)[None, :]).astype(BF16)
    cmp_start = jnp.arange(n_str) * CMP_STRIDE
    slc_start = jnp.arange(LANE) * SLC_BLOCK
    ovt_mat = ((cmp_start[None, :] < slc_start[:, None] + SLC_BLOCK)
               & (cmp_start[None, :] + CMP_LEN > slc_start[:, None])).astype(BF16)
    ovt_mat = jnp.concatenate([ovt_mat, jnp.ones((SUBLANE, n_str), BF16)], axis=0)
    ob = _nsa_call(p3, kc, vc, e_mat, ovt_mat, _rope_tables(pos, B_HEAD_DIM))

    wbr = w_branch[0].astype(BF16)
    a_w = A_HEADS * A_V_DIM
    out = _out_call(oa.reshape(t, a_w), ob.reshape(t, B_HEADS * B_HEAD_DIM), p2, x2, gate,
                    wbr[:a_w], wbr[a_w:], w_out[0].astype(BF16), final_norm_g[None, :], s)
    return out.reshape(b, s, d)
```

```python
import functools
import math

import jax
import jax.numpy as jnp
from jax import lax
from jax.experimental import pallas as pl
from jax.experimental.pallas import tpu as pltpu

F32 = jnp.float32
BF16 = jnp.bfloat16

D_MODEL = 2048
A_HEADS = 8
A_QK_DIM = 64
A_V_DIM = 128
B_HEADS = 8
B_GROUPS = 2
B_HPG = B_HEADS // B_GROUPS
B_HEAD_DIM = 128
CMP_LEN = 32
CMP_STRIDE = 16
CMP_HIDDEN = 256
SLC_BLOCK = 64
SLC_TOPK = 16
N_FORCED = 3
WINDOW = 512
FORCE_SCORE = 1.0e4
ROPE_THETA = 500000.0
ROPE_FRACTION = 4
EPS = 1e-6
NEG = -1e30
TAKEN = -3.0e38
LAM_INIT = 0.8 - 0.6 * math.exp(-0.3 * 0)
LOG2E = math.log2(math.e)

LANE = 128
SUBLANE = 8
VMEM_LIMIT = 56 * 1024 * 1024

CB_GA, CB_GB, CB_BGATE, CB_MAIN = 0, 16, 32, 36
CB_AQ, CB_AK, CB_AV, CB_AZ, CB_BQ = (CB_MAIN + o for o in (0, 8, 16, 24, 32))
CB_BKC, CB_BVC, CB_BKS, CB_BVS, CB_BKW, CB_BVW, CB_BZ = (CB_MAIN + o for o in (40, 42, 44, 46, 48, 50, 52))
P_WIDTH = 96 * LANE
PROJ_TN = 1536
N_TAIL_TILES = (CB_MAIN * LANE) // PROJ_TN
W_MAIN_COLS = (CB_BZ + 8 - CB_MAIN) * LANE
A_SCALE = LOG2E * A_QK_DIM ** -0.5
B_SCALE = LOG2E * B_HEAD_DIM ** -0.5


def _dot(a, b):
    return jnp.dot(a, b, preferred_element_type=F32)


def _dot_nt(a, b):
    return lax.dot_general(a, b, (((1,), (1,)), ((), ())), preferred_element_type=F32)


def _silu(v):
    return v * jax.nn.sigmoid(v)


def _rope(x, tabs, half):
    c, s1, s2 = tabs
    return x * c + pltpu.roll(x, LANE - half, axis=1) * s1 + pltpu.roll(x, half, axis=1) * s2


def _ada_kernel(c_ref, w_ref, b_ref, o_ref):
    o_ref[...] = _dot(_silu(c_ref[...]), w_ref[...]) + b_ref[...]


def _ada_call(c_pad, w, b):
    rows, d = c_pad.shape
    n = w.shape[1]
    tn = 768
    return pl.pallas_call(
        _ada_kernel,
        grid=(n // tn,),
        in_specs=[pl.BlockSpec((rows, d), lambda j: (0, 0)),
                  pl.BlockSpec((d, tn), lambda j: (0, j)),
                  pl.BlockSpec((1, tn), lambda j: (0, j))],
        out_specs=pl.BlockSpec((rows, tn), lambda j: (0, j)),
        out_shape=jax.ShapeDtypeStruct((rows, n), F32),
        compiler_params=pltpu.CompilerParams(vmem_limit_bytes=VMEM_LIMIT),
        name="ada",
    )(c_pad, w, b)


def _inproj_kernel(x_ref, g_ref, sc_ref, sh_ref, wt_ref, wm_ref, o_ref, h_ref):
    j = pl.program_id(1)

    @pl.when(j == 0)
    def _():
        x = x_ref[...]
        ms = jnp.mean(x * x, axis=-1, keepdims=True)
        xn = x * lax.rsqrt(ms + EPS)
        h_ref[...] = (xn * (g_ref[...] * (1.0 + sc_ref[...])) + sh_ref[...]).astype(h_ref.dtype)

    @pl.when(j < N_TAIL_TILES)
    def _():
        o_ref[...] = _dot(h_ref[...], wt_ref[...]).astype(o_ref.dtype)

    @pl.when(j >= N_TAIL_TILES)
    def _():
        o_ref[...] = _dot(h_ref[...], wm_ref[...]).astype(o_ref.dtype)


def _inproj_call(x2, g, scale, shift, w_tail, w_main, seq):
    t, d = x2.shape
    tm = min(1024, seq)
    per_seq = seq // tm
    mod = pl.BlockSpec((None, 1, d), lambda i, j: (i // per_seq, 0, 0))
    return pl.pallas_call(
        _inproj_kernel,
        grid=(t // tm, P_WIDTH // PROJ_TN),
        in_specs=[pl.BlockSpec((tm, d), lambda i, j: (i, 0)),
                  pl.BlockSpec((1, d), lambda i, j: (0, 0)),
                  mod, mod,
                  pl.BlockSpec((d, PROJ_TN), lambda i, j: (0, jnp.minimum(j, N_TAIL_TILES - 1))),
                  pl.BlockSpec((d, PROJ_TN), lambda i, j: (0, jnp.maximum(j - N_TAIL_TILES, 0)))],
        out_specs=pl.BlockSpec((tm, PROJ_TN), lambda i, j: (i, j)),
        out_shape=jax.ShapeDtypeStruct((t, P_WIDTH), BF16),
        scratch_shapes=[pltpu.VMEM((tm, d), BF16)],
        compiler_params=pltpu.CompilerParams(
            dimension_semantics=("arbitrary", "arbitrary"), vmem_limit_bytes=VMEM_LIMIT),
        name="inproj",
    )(x2, g, scale, shift, w_tail, w_main)


def _flash_init(m_ref, acc_ref):
    m_ref[...] = jnp.full(m_ref.shape, NEG, F32)
    acc_ref[...] = jnp.zeros(acc_ref.shape, F32)


def _with_ones(v):
    return jnp.concatenate([v, jnp.ones(v.shape, v.dtype)], axis=1)


def _flash_update(s, v, m_ref, acc_ref, first=False):
    m_cur = jnp.max(s, axis=-1, keepdims=True)
    if first:
        m_new = jnp.broadcast_to(m_cur, m_ref.shape)
    else:
        m_old = m_ref[...]
        m_new = jnp.maximum(m_old, m_cur)
        alpha = jnp.exp2(m_old - m_new)
    p = jnp.exp2(s - jnp.tile(m_new, (1, s.shape[1] // LANE)))
    pv = _dot(p.astype(v.dtype), _with_ones(v))
    acc_ref[...] = pv if first else jnp.tile(alpha, (1, 2)) * acc_ref[...] + pv
    m_ref[...] = m_new


def _flash_result(acc_ref):
    acc = acc_ref[...]
    return acc[:, 0:LANE] / acc[:, LANE:2 * LANE]


def _masked_attend(s, mask, v):
    s = jnp.where(mask, s, NEG)
    p = jnp.exp2(s - jnp.max(s, axis=-1, keepdims=True))
    pv = _dot(p.astype(v.dtype), _with_ones(v))
    return pv[:, 0:LANE] / pv[:, LANE:2 * LANE]


def _run_tiles(first, last, scores, update, bufs, before_last=None):
    n = last - first + 1
    nq = (n - 1) // 4

    def quad(qi, carry):
        j = first + 4 * qi
        for k in range(4):
            scores(j + k + 1, bufs[(k + 1) % 2])
            update(j + k, bufs[k % 2])
        return carry

    lax.fori_loop(0, nq, quad, 0)
    base = first + 4 * nq
    rem = n - 4 * nq

    def tail(r):
        for k in range(r - 1):
            scores(base + k + 1, bufs[(k + 1) % 2])
            update(base + k, bufs[k % 2])
        if before_last is not None:
            before_last()
        update(base + r - 1, bufs[(r - 1) % 2], True)

    for r in range(1, 5):
        pl.when(rem == r)(functools.partial(tail, r))


def _diff_kernel(q_ref, qn_ref, k_ref, v_ref, z_ref, c_ref, s1_ref, s2_ref, cn_ref, s1n_ref, s2n_ref,
                 lq1_ref, lk1_ref, lq2_ref, lk2_ref, g_ref, o_ref,
                 qs_ref, qsn_ref, kr_ref, sa_ref, sb_ref, sc_ref, m_ref, acc_ref, *, tq):
    i = pl.program_id(2)
    half = A_QK_DIM // ROPE_FRACTION // 2

    def build_qs(src_ref, tab_refs, dst_ref):
        q = _rope(src_ref[...].astype(F32), tuple(r[...] for r in tab_refs), half) * A_SCALE
        lane = lax.broadcasted_iota(jnp.int32, q.shape, 1)
        dst_ref[0:tq, :] = jnp.where(lane < A_QK_DIM, q, 0.0).astype(dst_ref.dtype)
        dst_ref[tq:2 * tq, :] = jnp.where(lane >= A_QK_DIM, q, 0.0).astype(dst_ref.dtype)

    def scores(j, s_ref):
        s_ref[...] = _dot_nt(qs_ref[...], kr_ref[pl.ds(pl.multiple_of(j * tq, tq), tq), :])

    def update(j, s_ref, causal=False, first=False):
        s = s_ref[...]
        if causal:
            row = lax.broadcasted_iota(jnp.int32, s.shape, 0)
            col = lax.broadcasted_iota(jnp.int32, s.shape, 1)
            s = jnp.where(col <= (row & (tq - 1)), s, NEG)
        _flash_update(s, v_ref[pl.ds(pl.multiple_of(j * tq, tq), tq), :], m_ref, acc_ref, first)

    def build_next():
        build_qs(qn_ref, (cn_ref, s1n_ref, s2n_ref), qsn_ref)

    def prefetch():
        sc_ref[...] = _dot_nt(qsn_ref[...], kr_ref[0:tq, :])

    def rope_keys():
        ktabs = (c_ref[...], s1_ref[...], s2_ref[...])
        kr_ref[pl.ds(pl.multiple_of(i * tq, tq), tq), :] = (
            _rope(k_ref[...].astype(F32), ktabs, half).astype(kr_ref.dtype))

    @pl.when(i == 0)
    def _():
        rope_keys()
        build_qs(q_ref, (c_ref, s1_ref, s2_ref), qs_ref)
        sc_ref[...] = _dot_nt(qs_ref[...], kr_ref[0:tq, :])
        build_next()
        update(0, sc_ref, True, True)
        prefetch()

    @pl.when(i > 0)
    def _():
        qs_ref[...] = qsn_ref[...]
        rope_keys()
        scores(1, sa_ref)
        build_next()
        update(0, sc_ref, False, True)
        _run_tiles(1, i, scores, update, (sa_ref, sb_ref), before_last=prefetch)

    o = _flash_result(acc_ref)
    lam = (jnp.exp(jnp.sum(lq1_ref[...] * lk1_ref[...], axis=-1, keepdims=True))
           - jnp.exp(jnp.sum(lq2_ref[...] * lk2_ref[...], axis=-1, keepdims=True)) + LAM_INIT)
    d = o[0:tq] - lam * o[tq:2 * tq]
    dn = d * lax.rsqrt(jnp.mean(d * d, axis=-1, keepdims=True) + EPS)
    dn = dn * g_ref[...] * (1.0 - LAM_INIT)
    o_ref[...] = (dn * _silu(z_ref[...].astype(F32))).astype(o_ref.dtype)


def _diff_call(p3, tabs, lq1, lk1, lq2, lk2, sub_g):
    b, s, _ = p3.shape
    tq = 512
    last = s // tq - 1
    lam_spec = pl.BlockSpec((1, A_QK_DIM), lambda bi, h, i: (0, 0))
    tab_spec = pl.BlockSpec((tq, LANE), lambda bi, h, i: (i, 0))
    tabn_spec = pl.BlockSpec((tq, LANE), lambda bi, h, i: (jnp.minimum(i + 1, last), 0))
    return pl.pallas_call(
        functools.partial(_diff_kernel, tq=tq),
        grid=(b, A_HEADS, s // tq),
        in_specs=[pl.BlockSpec((None, tq, LANE), lambda bi, h, i: (bi, i, CB_AQ + h)),
                  pl.BlockSpec((None, tq, LANE), lambda bi, h, i: (bi, jnp.minimum(i + 1, last), CB_AQ + h)),
                  pl.BlockSpec((None, tq, LANE), lambda bi, h, i: (bi, i, CB_AK + h)),
                  pl.BlockSpec((None, s, LANE), lambda bi, h, i: (bi, 0, CB_AV + h)),
                  pl.BlockSpec((None, tq, LANE), lambda bi, h, i: (bi, i, CB_AZ + h)),
                  tab_spec, tab_spec, tab_spec, tabn_spec, tabn_spec, tabn_spec,
                  lam_spec, lam_spec, lam_spec, lam_spec,
                  pl.BlockSpec((1, A_V_DIM), lambda bi, h, i: (0, 0))],
        out_specs=pl.BlockSpec((None, tq, LANE), lambda bi, h, i: (bi, i, h)),
        out_shape=jax.ShapeDtypeStruct((b, s, A_HEADS * A_V_DIM), BF16),
        scratch_shapes=[pltpu.VMEM((2 * tq, LANE), BF16),
                        pltpu.VMEM((2 * tq, LANE), BF16),
                        pltpu.VMEM((s, LANE), BF16),
                        pltpu.VMEM((2 * tq, tq), F32),
                        pltpu.VMEM((2 * tq, tq), F32),
                        pltpu.VMEM((2 * tq, tq), F32),
                        pltpu.VMEM((2 * tq, LANE), F32),
                        pltpu.VMEM((2 * tq, 2 * LANE), F32)],
        compiler_params=pltpu.CompilerParams(
            dimension_semantics=("arbitrary", "arbitrary", "arbitrary"), vmem_limit_bytes=VMEM_LIMIT),
        name="diff_attn",
    )(p3, p3, p3, p3, p3, *tabs, *tabs, lq1, lk1, lq2, lk2, sub_g)


def _compress_kernel(xk_ref, xv_ref, pek_ref, pev_ref, w1k_ref, w2k_ref, w1v_ref, w2v_ref,
                     c_ref, s1_ref, s2_ref, kc_ref, vc_ref):
    n = xk_ref.shape[0]
    hd = B_HEAD_DIM
    half_w = CMP_STRIDE * hd
    tabs = (c_ref[...], s1_ref[...], s2_ref[...])

    def mlp(x_ref, g, pe_ref, w1_ref, w2_ref):
        x = jnp.concatenate([x_ref[:, (r * B_GROUPS + g) * hd:(r * B_GROUPS + g + 1) * hd]
                             for r in range(CMP_STRIDE)], axis=1).astype(F32)
        top = _dot((x + pe_ref[0:1, :]).astype(BF16), w1_ref[0:half_w, :])
        bot = _dot((x + pe_ref[1:2, :]).astype(BF16), w1_ref[half_w:2 * half_w, :])
        hid = top + pltpu.roll(bot, n - 1, axis=0)
        return _dot(_silu(hid).astype(BF16), w2_ref[...])

    for g in range(B_GROUPS):
        kc = mlp(xk_ref, g, pek_ref, w1k_ref, w2k_ref)
        kc_ref[g] = _rope(kc, tabs, hd // ROPE_FRACTION // 2).astype(kc_ref.dtype)
        vc_ref[g] = mlp(xv_ref, g, pev_ref, w1v_ref, w2v_ref).astype(vc_ref.dtype)


def _compress_call(xk, xv, pek, pev, w1k, w2k, w1v, w2v, tabs):
    b, n, xw = xk.shape
    x_spec = pl.BlockSpec((None, n, xw), lambda bi: (bi, 0, 0))
    o_spec = pl.BlockSpec((None, B_GROUPS, n, B_HEAD_DIM), lambda bi: (bi, 0, 0, 0))

    def full(a):
        return pl.BlockSpec(a.shape, lambda bi: (0,) * a.ndim)

    consts = (pek, pev, w1k, w2k, w1v, w2v, *tabs)
    return pl.pallas_call(
        _compress_kernel,
        grid=(b,),
        in_specs=[x_spec, x_spec] + [full(a) for a in consts],
        out_specs=[o_spec, o_spec],
        out_shape=[jax.ShapeDtypeStruct((b, B_GROUPS, n, B_HEAD_DIM), BF16)] * 2,
        compiler_params=pltpu.CompilerParams(
            dimension_semantics=("arbitrary",), vmem_limit_bytes=VMEM_LIMIT),
        name="compress",
    )(xk, xv, *consts)


def _select_blocks(imp_t, s0, n_slc, n_top):
    shape = imp_t.shape
    jb = lax.broadcasted_iota(jnp.int32, shape, 0)
    jbf = jb.astype(F32)
    qpos = s0 + lax.broadcasted_iota(jnp.int32, shape, 1)
    cur = qpos // SLC_BLOCK
    valid = (jb * SLC_BLOCK <= qpos) & (jb < n_slc)
    forced = (jb == 0) | (jb == cur) | (jb == cur - 1)
    imp = jnp.where(valid, jnp.where(forced, TAKEN, imp_t), NEG)
    for _ in range(n_top - N_FORCED):
        mx = jnp.max(imp, axis=0, keepdims=True)
        idx = jnp.min(jnp.where(imp == mx, jbf, float(LANE)), axis=0, keepdims=True)
        imp = jnp.where(jbf == idx, TAKEN, imp)
    return jnp.where((imp == TAKEN) & valid, 0.0, NEG)


def _nsa_kernel(q_ref, kc_ref, vc_ref, ks_ref, vs_ref, kw_ref, vw_ref, e_ref, ovt_ref, wbias_ref, gate_ref, z_ref,
                c_ref, s1_ref, s2_ref, o_ref, qa_ref, ksr_ref, kwr_ref, sa_ref, sb_ref, m_ref, acc_ref,
                *, tq, tk, n_slc, n_top):
    i = pl.program_id(2)
    s0 = i * tq
    hd = B_HEAD_DIM
    rows = B_HPG * tq
    half = hd // ROPE_FRACTION // 2

    sub = pl.multiple_of((i % (tk // tq)) * tq, tq)
    qtabs = tuple(r[pl.ds(sub, tq), :] for r in (c_ref, s1_ref, s2_ref))
    qh = [(_rope(q_ref[:, h * hd:(h + 1) * hd].astype(F32), qtabs, half) * B_SCALE).astype(BF16)
          for h in range(B_HPG)]
    qs = jnp.concatenate(qh, axis=0)

    @pl.when(i % (tk // tq) == 0)
    def _():
        ktabs = (c_ref[...], s1_ref[...], s2_ref[...])
        k0 = pl.multiple_of(s0, tk)
        ksr_ref[pl.ds(k0, tk), :] = _rope(ks_ref[...].astype(F32), ktabs, half).astype(BF16)
        kwr_ref[pl.ds(WINDOW + k0, tk), :] = _rope(kw_ref[...].astype(F32), ktabs, half).astype(BF16)

    @pl.when(i == 0)
    def _():
        kwr_ref[0:WINDOW, :] = jnp.zeros((WINDOW, LANE), BF16)

    rid = lax.broadcasted_iota(jnp.int32, (rows, 1), 0)
    qpos_r = s0 + (rid & (tq - 1))

    n_cmp_pad = kc_ref.shape[0]
    cend = lax.broadcasted_iota(jnp.int32, (1, n_cmp_pad), 1) * CMP_STRIDE + (CMP_LEN - 1)
    sc = jnp.where(cend <= qpos_r, _dot_nt(qs, kc_ref[...]), NEG)
    pc = jnp.exp2(sc - jnp.max(sc, axis=-1, keepdims=True)).astype(BF16)
    pv = _dot(pc, _with_ones(vc_ref[...]))
    o_cmp = pv[:, 0:LANE] * jnp.where(qpos_r >= CMP_LEN - 1, 1.0 / pv[:, LANE:2 * LANE], 0.0)
    qpos_t = s0 + lax.broadcasted_iota(jnp.int32, (1, tq), 1)
    imp_t = jnp.zeros((LANE, tq), F32)
    for h in range(B_HPG):
        r = _dot_nt(ovt_ref[...], pc[h * tq:(h + 1) * tq])
        imp_t = imp_t + r[0:LANE] * jnp.where(qpos_t >= CMP_LEN - 1, 1.0 / r[LANE:LANE + 1], 0.0)

    kw_win = kwr_ref[pl.ds(pl.multiple_of(s0, tq), 2 * tq), :]
    s_win = [_dot_nt(qh[h], kw_win) for h in range(B_HPG)]

    selb16 =_select_blocks(imp_t, s0, n_slc, n_top).T.astype(BF16)
    for h in range(B_HPG):
        qa_ref[h * tq:(h + 1) * tq, 0:hd] = qh[h]
        qa_ref[h * tq:(h + 1) * tq, hd:2 * hd] = selb16
    _flash_init(m_ref, acc_ref)

    def slc_scores(j, s_ref):
        k0 = pl.multiple_of(j * tk, tk)
        ka = jnp.concatenate([ksr_ref[pl.ds(k0, tk), :], e_ref[pl.ds(k0, tk), :]], axis=1)
        s_ref[...] = _dot_nt(qa_ref[...], ka)

    def slc_update(j, s_ref, causal=False):
        k0 = pl.multiple_of(j * tk, tk)
        s = s_ref[...]
        if causal:
            kpos = k0 + lax.broadcasted_iota(jnp.int32, (1, tk), 1)
            s = jnp.where(kpos <= qpos_r, s, NEG)
        _flash_update(s, vs_ref[pl.ds(k0, tk), :], m_ref, acc_ref)

    slc_scores(0, sa_ref)

    bias_lo = wbias_ref[jnp.where(i == 0, 2, 0)]
    bias_hi = wbias_ref[1]
    v_lo = vw_ref[pl.ds(pl.multiple_of(jnp.maximum(s0 - WINDOW, 0), tq), tq), :]
    v_win = _with_ones(jnp.concatenate([v_lo, vw_ref[pl.ds(pl.multiple_of(s0, tq), tq), :]], axis=0))
    o_win = []
    for h in range(B_HPG):
        sw = jnp.concatenate([s_win[h][:, 0:tq] + bias_lo, s_win[h][:, tq:2 * tq] + bias_hi], axis=1)
        pw = jnp.exp2(sw - jnp.max(sw, axis=-1, keepdims=True))
        pvw = _dot(pw.astype(BF16), v_win)
        o_win.append(pvw[:, 0:LANE] / pvw[:, LANE:2 * LANE])

    _run_tiles(0, s0 // tk, slc_scores, slc_update, (sa_ref, sb_ref))
    o_slc = _flash_result(acc_ref)

    gates = jax.nn.sigmoid(gate_ref[...].astype(F32))
    for h in range(B_HPG):
        r = slice(h * tq, (h + 1) * tq)
        o = (gates[:, 3 * h:3 * h + 1] * o_cmp[r] + gates[:, 3 * h + 1:3 * h + 2] * o_slc[r]
             + gates[:, 3 * h + 2:3 * h + 3] * o_win[h])
        z = z_ref[:, h * hd:(h + 1) * hd].astype(F32)
        o_ref[:, h * hd:(h + 1) * hd] = (o * _silu(z)).astype(o_ref.dtype)


def _nsa_call(p3, kc, vc, e_mat, ovt_mat, wbias, tabs):
    b, s, _ = p3.shape
    tq, tk = 512, 512
    assert tq == WINDOW
    n_slc = s // SLC_BLOCK
    n_top = min(SLC_TOPK, n_slc)
    gw = B_HPG * B_HEAD_DIM
    rows = B_HPG * tq
    n_cmp_pad = kc.shape[2]
    per_k = tk // tq

    def slab(cb):
        return pl.BlockSpec((None, s, LANE), lambda bi, g, i: (bi, 0, cb + g))

    def ktile(cb):
        return pl.BlockSpec((None, tk, LANE), lambda bi, g, i: (bi, i // per_k, cb + g))

    cmp_spec = pl.BlockSpec((None, None, n_cmp_pad, B_HEAD_DIM), lambda bi, g, i: (bi, g, 0, 0))
    tab_spec = pl.BlockSpec((tk, LANE), lambda bi, g, i: (i // per_k, 0))
    return pl.pallas_call(
        functools.partial(_nsa_kernel, tq=tq, tk=tk, n_slc=n_slc, n_top=n_top),
        grid=(b, B_GROUPS, s // tq),
        in_specs=[pl.BlockSpec((None, tq, gw), lambda bi, g, i: (bi, i, CB_BQ * LANE // gw + g)),
                  cmp_spec, cmp_spec,
                  ktile(CB_BKS), slab(CB_BVS), ktile(CB_BKW), slab(CB_BVW),
                  pl.BlockSpec((s, LANE), lambda bi, g, i: (0, 0), pipeline_mode=pl.Buffered(1)),
                  pl.BlockSpec((LANE + SUBLANE, n_cmp_pad), lambda bi, g, i: (0, 0), pipeline_mode=pl.Buffered(1)),
                  pl.BlockSpec(wbias.shape, lambda bi, g, i: (0, 0, 0), pipeline_mode=pl.Buffered(1)),
                  pl.BlockSpec((None, tq, LANE), lambda bi, g, i: (bi, i, CB_BGATE + g)),
                  pl.BlockSpec((None, tq, gw), lambda bi, g, i: (bi, i, CB_BZ * LANE // gw + g)),
                  tab_spec, tab_spec, tab_spec],
        out_specs=pl.BlockSpec((None, tq, gw), lambda bi, g, i: (bi, i, g)),
        out_shape=jax.ShapeDtypeStruct((b, s, B_HEADS * B_HEAD_DIM), BF16),
        scratch_shapes=[pltpu.VMEM((rows, 2 * B_HEAD_DIM), BF16),
                        pltpu.VMEM((s, LANE), BF16),
                        pltpu.VMEM((s + WINDOW, LANE), BF16),
                        pltpu.VMEM((rows, tk), F32),
                        pltpu.VMEM((rows, tk), F32),
                        pltpu.VMEM((rows, LANE), F32),
                        pltpu.VMEM((rows, 2 * LANE), F32)],
        compiler_params=pltpu.CompilerParams(
            dimension_semantics=("arbitrary", "arbitrary", "arbitrary"), vmem_limit_bytes=VMEM_LIMIT),
        name="nsa",
    )(p3, kc, vc, p3, p3, p3, p3, e_mat, ovt_mat, wbias, p3, p3, *tabs)


def _out_kernel(oa_ref, ob_ref, ga_ref, gb_ref, x_ref, gate_ref, wa_ref, wb_ref, wo_ref, fg_ref, o_ref):
    ya = _dot(oa_ref[...], wa_ref[...])
    yb = _dot(ob_ref[...], wb_ref[...])
    mix = (jax.nn.sigmoid(ga_ref[...].astype(F32)) * ya
           + jax.nn.sigmoid(gb_ref[...].astype(F32)) * yb)
    y = _dot(mix.astype(BF16), wo_ref[...])
    xo = x_ref[...] + gate_ref[...] * y
    o_ref[...] = xo * lax.rsqrt(jnp.mean(xo * xo, axis=-1, keepdims=True) + EPS) * fg_ref[...]


def _out_call(oa, ob, p2, x2, gate, wa, wb, wo, fg, seq):
    t, d = x2.shape
    tm = 256
    per_seq = seq // tm
    aw = oa.shape[1]
    bw = ob.shape[1]

    def resident(a):
        return pl.BlockSpec(a.shape, lambda i: (0, 0), pipeline_mode=pl.Buffered(1))

    return pl.pallas_call(
        _out_kernel,
        grid=(t // tm,),
        in_specs=[pl.BlockSpec((tm, aw), lambda i: (i, 0)),
                  pl.BlockSpec((tm, bw), lambda i: (i, 0)),
                  pl.BlockSpec((tm, d), lambda i: (i, CB_GA * LANE // d)),
                  pl.BlockSpec((tm, d), lambda i: (i, CB_GB * LANE // d)),
                  pl.BlockSpec((tm, d), lambda i: (i, 0)),
                  pl.BlockSpec((None, 1, d), lambda i: (i // per_seq, 0, 0)),
                  resident(wa), resident(wb), resident(wo),
                  pl.BlockSpec((1, d), lambda i: (0, 0))],
        out_specs=pl.BlockSpec((tm, d), lambda i: (i, 0)),
        out_shape=jax.ShapeDtypeStruct((t, d), F32),
        compiler_params=pltpu.CompilerParams(
            dimension_semantics=("arbitrary",), vmem_limit_bytes=VMEM_LIMIT),
        name="out_proj",
    )(oa, ob, p2, p2, x2, gate, wa, wb, wo, fg)


def _rope_tables(pos, head_dim):
    rd = head_dim // ROPE_FRACTION
    half = rd // 2
    inv = 1.0 / (ROPE_THETA ** (jnp.arange(half, dtype=F32) * (2.0 / rd)))
    ang = pos.astype(F32)[:, None] * inv[None, :]
    cos, sin = jnp.cos(ang), jnp.sin(ang)
    n = pos.shape[0]
    pad = jnp.zeros((n, head_dim - rd), F32)
    zero = jnp.zeros((n, half), F32)
    c = jnp.concatenate([cos, cos, pad + 1.0], axis=1)
    s1 = jnp.concatenate([-sin, zero, pad], axis=1)
    s2 = jnp.concatenate([zero, sin, pad], axis=1)
    reps = LANE // head_dim
    return tuple(jnp.tile(a, (1, reps)) for a in (c, s1, s2))


def _gate_weights(w):
    d = w.shape[0]
    gpg = B_HPG * 3
    a_g = W_MAIN_COLS
    a_m = a_g + B_GROUPS * gpg
    zpad = jnp.zeros((d, LANE - gpg), w.dtype)
    return jnp.concatenate([w[:, a_m:a_m + 2 * D_MODEL],
                            w[:, a_g:a_g + gpg], zpad, w[:, a_g + gpg:a_g + 2 * gpg], zpad,
                            jnp.zeros((d, 2 * LANE), w.dtype)], axis=1)


def kernel(x, c, w_ada, b_ada, norm_g, w_in, lambda_q1, lambda_k1, lambda_q2, lambda_k2, diff_norm_g,
           cmp_pe_k, cmp_pe_v, cmp_w1_k, cmp_w2_k, cmp_w1_v, cmp_w2_v, w_branch, w_out, final_norm_g):
    b, s, d = x.shape
    assert d == D_MODEL and s % 1024 == 0 and s // SLC_BLOCK <= LANE
    t = b * s
    x2 = x.reshape(t, d)

    c_pad = jnp.pad(c, ((0, 8 - b % 8 if b % 8 else 0), (0, 0)))
    mod = _ada_call(c_pad, w_ada[0], b_ada[0][None, :])[:b]
    shift = mod[:, None, 0:d]
    scale = mod[:, None, d:2 * d]
    gate = mod[:, None, 2 * d:3 * d]

    w_bf = w_in[0].astype(BF16)
    p2 = _inproj_call(x2, norm_g[0][None, :], scale, shift, _gate_weights(w_bf), w_bf, s)
    p3 = p2.reshape(b, s, P_WIDTH)

    pos = jnp.arange(s)
    oa = _diff_call(p3, _rope_tables(pos, A_QK_DIM), lambda_q1, lambda_k1, lambda_q2, lambda_k2, diff_norm_g)

    n_str = s // CMP_STRIDE
    hw = CMP_STRIDE * B_HEAD_DIM

    def strides(cb):
        return p3[:, :, cb * LANE:(cb + B_GROUPS) * LANE].reshape(b, n_str, CMP_STRIDE * B_GROUPS * B_HEAD_DIM)

    cmp_end = jnp.arange(n_str) * CMP_STRIDE + (CMP_LEN - 1)
    kc, vc = _compress_call(
        strides(CB_BKC), strides(CB_BVC),
        cmp_pe_k[0].reshape(2, hw), cmp_pe_v[0].reshape(2, hw),
        cmp_w1_k[0].astype(BF16), cmp_w2_k[0].astype(BF16),
        cmp_w1_v[0].astype(BF16), cmp_w2_v[0].astype(BF16),
        _rope_tables(cmp_end, B_HEAD_DIM))
    e_mat = (jnp.arange(s)[:, None] // SLC_BLOCK == jnp.arange(LANE)[None, :]).astype(BF16)
    cmp_start = jnp.arange(n_str) * CMP_STRIDE
    slc_start = jnp.arange(LANE) * SLC_BLOCK
    ovt_mat = ((cmp_start[None, :] < slc_start[:, None] + SLC_BLOCK)
               & (cmp_start[None, :] + CMP_LEN > slc_start[:, None])).astype(BF16)
    ovt_mat = jnp.concatenate([ovt_mat, jnp.ones((SUBLANE, n_str), BF16)], axis=0)
    tri = jnp.arange(WINDOW)[None, :] - jnp.arange(WINDOW)[:, None]
    wbias = jnp.stack([jnp.where(tri > 0, 0.0, NEG), jnp.where(tri <= 0, 0.0, NEG),
                       jnp.full((WINDOW, WINDOW), NEG)]).astype(F32)
    ob = _nsa_call(p3, kc, vc, e_mat, ovt_mat, wbias, _rope_tables(pos, B_HEAD_DIM))

    wbr = w_branch[0].astype(BF16)
    a_w = A_HEADS * A_V_DIM
    out = _out_call(oa.reshape(t, a_w), ob.reshape(t, B_HEADS * B_HEAD_DIM), p2, x2, gate,
                    wbr[:a_w], wbr[a_w:], w_out[0].astype(BF16), final_norm_g[None, :], s)
    return out.reshape(b, s, d)
```

```python
import functools
import math

import jax
import jax.numpy as jnp
from jax import lax
from jax.experimental import pallas as pl
from jax.experimental.pallas import tpu as pltpu

F32 = jnp.float32
BF16 = jnp.bfloat16

D_MODEL = 2048
A_HEADS = 8
A_QK_DIM = 64
A_V_DIM = 128
B_HEADS = 8
B_GROUPS = 2
B_HPG = B_HEADS // B_GROUPS
B_HEAD_DIM = 128
CMP_LEN = 32
CMP_STRIDE = 16
CMP_HIDDEN = 256
SLC_BLOCK = 64
SLC_TOPK = 16
N_FORCED = 3
WINDOW = 512
FORCE_SCORE = 1.0e4
ROPE_THETA = 500000.0
ROPE_FRACTION = 4
EPS = 1e-6
NEG = -1e30
TAKEN = -3.0e38
LAM_INIT = 0.8 - 0.6 * math.exp(-0.3 * 0)
LOG2E = math.log2(math.e)

LANE = 128
SUBLANE = 8
VMEM_LIMIT = 56 * 1024 * 1024

CB_GA, CB_GB, CB_BGATE, CB_MAIN = 0, 16, 32, 36
CB_AQ, CB_AK, CB_AV, CB_AZ, CB_BQ = (CB_MAIN + o for o in (0, 8, 16, 24, 32))
CB_BKC, CB_BVC, CB_BKS, CB_BVS, CB_BKW, CB_BVW, CB_BZ = (CB_MAIN + o for o in (40, 42, 44, 46, 48, 50, 52))
P_WIDTH = 96 * LANE
PROJ_TN = 1536
N_TAIL_TILES = (CB_MAIN * LANE) // PROJ_TN
W_MAIN_COLS = (CB_BZ + 8 - CB_MAIN) * LANE
A_SCALE = LOG2E * A_QK_DIM ** -0.5
B_SCALE = LOG2E * B_HEAD_DIM ** -0.5


def _dot(a, b):
    return jnp.dot(a, b, preferred_element_type=F32)


def _dot_nt(a, b):
    return lax.dot_general(a, b, (((1,), (1,)), ((), ())), preferred_element_type=F32)


def _silu(v):
    return v * jax.nn.sigmoid(v)


def _rope(x, tabs, half):
    c, s1, s2 = tabs
    return x * c + pltpu.roll(x, LANE - half, axis=1) * s1 + pltpu.roll(x, half, axis=1) * s2


def _rope_mxu(x16, tabs, perm):
    c, s1, s2 = tabs
    return x16.astype(F32) * c + _dot(x16, perm) * (s2 - s1)


def _ada_kernel(c_ref, w_ref, b_ref, o_ref):
    o_ref[...] = _dot(_silu(c_ref[...]), w_ref[...]) + b_ref[...]


def _ada_call(c_pad, w, b):
    rows, d = c_pad.shape
    n = w.shape[1]
    tn = 768
    return pl.pallas_call(
        _ada_kernel,
        grid=(n // tn,),
        in_specs=[pl.BlockSpec((rows, d), lambda j: (0, 0)),
                  pl.BlockSpec((d, tn), lambda j: (0, j)),
                  pl.BlockSpec((1, tn), lambda j: (0, j))],
        out_specs=pl.BlockSpec((rows, tn), lambda j: (0, j)),
        out_shape=jax.ShapeDtypeStruct((rows, n), F32),
        compiler_params=pltpu.CompilerParams(vmem_limit_bytes=VMEM_LIMIT),
        name="ada",
    )(c_pad, w, b)


def _inproj_kernel(x_ref, g_ref, sc_ref, sh_ref, wt_ref, wm_ref, o_ref, h_ref):
    j = pl.program_id(1)

    @pl.when(j == 0)
    def _():
        x = x_ref[...]
        ms = jnp.mean(x * x, axis=-1, keepdims=True)
        xn = x * lax.rsqrt(ms + EPS)
        h_ref[...] = (xn * (g_ref[...] * (1.0 + sc_ref[...])) + sh_ref[...]).astype(h_ref.dtype)

    @pl.when(j < N_TAIL_TILES)
    def _():
        o_ref[...] = _dot(h_ref[...], wt_ref[...]).astype(o_ref.dtype)

    @pl.when(j >= N_TAIL_TILES)
    def _():
        o_ref[...] = _dot(h_ref[...], wm_ref[...]).astype(o_ref.dtype)


def _inproj_call(x2, g, scale, shift, w_tail, w_main, seq):
    t, d = x2.shape
    tm = min(1024, seq)
    per_seq = seq // tm
    mod = pl.BlockSpec((None, 1, d), lambda i, j: (i // per_seq, 0, 0))
    return pl.pallas_call(
        _inproj_kernel,
        grid=(t // tm, P_WIDTH // PROJ_TN),
        in_specs=[pl.BlockSpec((tm, d), lambda i, j: (i, 0)),
                  pl.BlockSpec((1, d), lambda i, j: (0, 0)),
                  mod, mod,
                  pl.BlockSpec((d, PROJ_TN), lambda i, j: (0, jnp.minimum(j, N_TAIL_TILES - 1))),
                  pl.BlockSpec((d, PROJ_TN), lambda i, j: (0, jnp.maximum(j - N_TAIL_TILES, 0)))],
        out_specs=pl.BlockSpec((tm, PROJ_TN), lambda i, j: (i, j)),
        out_shape=jax.ShapeDtypeStruct((t, P_WIDTH), BF16),
        scratch_shapes=[pltpu.VMEM((tm, d), BF16)],
        compiler_params=pltpu.CompilerParams(
            dimension_semantics=("arbitrary", "arbitrary"), vmem_limit_bytes=VMEM_LIMIT),
        name="inproj",
    )(x2, g, scale, shift, w_tail, w_main)


def _flash_init(m_ref, acc_ref):
    m_ref[...] = jnp.full(m_ref.shape, NEG, F32)
    acc_ref[...] = jnp.zeros(acc_ref.shape, F32)


def _with_ones(v):
    return jnp.concatenate([v, jnp.ones(v.shape, v.dtype)], axis=1)


def _flash_update(s, v, m_ref, acc_ref, first=False):
    m_cur = jnp.max(s, axis=-1, keepdims=True)
    if first:
        m_new = jnp.broadcast_to(m_cur, m_ref.shape)
    else:
        m_old = m_ref[...]
        m_new = jnp.maximum(m_old, m_cur)
        alpha = jnp.exp2(m_old - m_new)
    p = jnp.exp2(s - jnp.tile(m_new, (1, s.shape[1] // LANE)))
    pv = _dot(p.astype(v.dtype), _with_ones(v))
    acc_ref[...] = pv if first else jnp.tile(alpha, (1, 2)) * acc_ref[...] + pv
    m_ref[...] = m_new


def _flash_result(acc_ref):
    acc = acc_ref[...]
    return acc[:, 0:LANE] / acc[:, LANE:2 * LANE]


def _masked_attend(s, mask, v):
    s = jnp.where(mask, s, NEG)
    p = jnp.exp2(s - jnp.max(s, axis=-1, keepdims=True))
    pv = _dot(p.astype(v.dtype), _with_ones(v))
    return pv[:, 0:LANE] / pv[:, LANE:2 * LANE]


def _run_tiles(first, last, scores, update, bufs, before_last=None, n_diag=1):
    n = last - first + 1
    nq = jnp.maximum((n - n_diag) // 4, 0)

    def quad(qi, carry):
        j = first + 4 * qi
        for k in range(4):
            scores(j + k + 1, bufs[(k + 1) % 2])
            update(j + k, bufs[k % 2], None)
        return carry

    lax.fori_loop(0, nq, quad, 0)
    base = first + 4 * nq
    rem = n - 4 * nq

    def tail(r):
        for k in range(r):
            if k + 1 < r:
                scores(base + k + 1, bufs[(k + 1) % 2])
            elif before_last is not None:
                before_last()
            d = n_diag - (r - k)
            update(base + k, bufs[k % 2], d if d >= 0 else None)

    for r in range(1, n_diag + 4):
        pl.when(rem == r)(functools.partial(tail, r))


def _diff_kernel(q_ref, qn_ref, k_ref, v_ref, z_ref, c_ref, s1_ref, s2_ref, cn_ref, s1n_ref, s2n_ref,
                 lq1_ref, lk1_ref, lq2_ref, lk2_ref, g_ref, o_ref,
                 qs_ref, qsn_ref, kr_ref, sa_ref, sb_ref, sc_ref, m_ref, acc_ref, *, tq, tk):
    i = pl.program_id(2)
    half = A_QK_DIM // ROPE_FRACTION // 2
    n_diag = tq // tk

    def build_qs(src_ref, tab_refs, dst_ref):
        q = _rope(src_ref[...].astype(F32), tuple(r[...] for r in tab_refs), half) * A_SCALE
        lane = lax.broadcasted_iota(jnp.int32, q.shape, 1)
        dst_ref[0:tq, :] = jnp.where(lane < A_QK_DIM, q, 0.0).astype(dst_ref.dtype)
        dst_ref[tq:2 * tq, :] = jnp.where(lane >= A_QK_DIM, q, 0.0).astype(dst_ref.dtype)

    def scores(j, s_ref):
        s_ref[...] = _dot_nt(qs_ref[...], kr_ref[pl.ds(pl.multiple_of(j * tk, tk), tk), :])

    def update(j, s_ref, d, first=False):
        s = s_ref[...]
        if d is not None:
            row = lax.broadcasted_iota(jnp.int32, s.shape, 0)
            col = lax.broadcasted_iota(jnp.int32, s.shape, 1)
            s = jnp.where(col + d * tk <= (row & (tq - 1)), s, NEG)
        _flash_update(s, v_ref[pl.ds(pl.multiple_of(j * tk, tk), tk), :], m_ref, acc_ref, first)

    def build_next():
        build_qs(qn_ref, (cn_ref, s1n_ref, s2n_ref), qsn_ref)

    def prefetch():
        sc_ref[...] = _dot_nt(qsn_ref[...], kr_ref[0:tk, :])

    def rope_keys():
        ktabs = (c_ref[...], s1_ref[...], s2_ref[...])
        kr_ref[pl.ds(pl.multiple_of(i * tq, tq), tq), :] = (
            _rope(k_ref[...].astype(F32), ktabs, half).astype(kr_ref.dtype))

    @pl.when(i == 0)
    def _():
        rope_keys()
        build_qs(q_ref, (c_ref, s1_ref, s2_ref), qs_ref)
        sc_ref[...] = _dot_nt(qs_ref[...], kr_ref[0:tk, :])
        for d in range(1, n_diag):
            scores(d, (sa_ref, sb_ref)[(d - 1) % 2])
        build_next()
        update(0, sc_ref, 0, True)
        prefetch()
        for d in range(1, n_diag):
            update(d, (sa_ref, sb_ref)[(d - 1) % 2], d)

    @pl.when(i > 0)
    def _():
        qs_ref[...] = qsn_ref[...]
        rope_keys()
        scores(1, sa_ref)
        build_next()
        update(0, sc_ref, None, True)
        _run_tiles(1, n_diag * (i + 1) - 1, scores, update, (sa_ref, sb_ref), before_last=prefetch, n_diag=n_diag)

    o = _flash_result(acc_ref)
    lam = (jnp.exp(jnp.sum(lq1_ref[...] * lk1_ref[...], axis=-1, keepdims=True))
           - jnp.exp(jnp.sum(lq2_ref[...] * lk2_ref[...], axis=-1, keepdims=True)) + LAM_INIT)
    d = o[0:tq] - lam * o[tq:2 * tq]
    dn = d * lax.rsqrt(jnp.mean(d * d, axis=-1, keepdims=True) + EPS)
    dn = dn * g_ref[...] * (1.0 - LAM_INIT)
    o_ref[...] = (dn * _silu(z_ref[...].astype(F32))).astype(o_ref.dtype)


def _diff_call(p3, tabs, lq1, lk1, lq2, lk2, sub_g):
    b, s, _ = p3.shape
    tq, tk = 512, 512
    last = s // tq - 1
    lam_spec = pl.BlockSpec((1, A_QK_DIM), lambda bi, h, i: (0, 0))
    tab_spec = pl.BlockSpec((tq, LANE), lambda bi, h, i: (i, 0))
    tabn_spec = pl.BlockSpec((tq, LANE), lambda bi, h, i: (jnp.minimum(i + 1, last), 0))
    return pl.pallas_call(
        functools.partial(_diff_kernel, tq=tq, tk=tk),
        grid=(b, A_HEADS, s // tq),
        in_specs=[pl.BlockSpec((None, tq, LANE), lambda bi, h, i: (bi, i, CB_AQ + h)),
                  pl.BlockSpec((None, tq, LANE), lambda bi, h, i: (bi, jnp.minimum(i + 1, last), CB_AQ + h)),
                  pl.BlockSpec((None, tq, LANE), lambda bi, h, i: (bi, i, CB_AK + h)),
                  pl.BlockSpec((None, s, LANE), lambda bi, h, i: (bi, 0, CB_AV + h)),
                  pl.BlockSpec((None, tq, LANE), lambda bi, h, i: (bi, i, CB_AZ + h)),
                  tab_spec, tab_spec, tab_spec, tabn_spec, tabn_spec, tabn_spec,
                  lam_spec, lam_spec, lam_spec, lam_spec,
                  pl.BlockSpec((1, A_V_DIM), lambda bi, h, i: (0, 0))],
        out_specs=pl.BlockSpec((None, tq, LANE), lambda bi, h, i: (bi, i, h)),
        out_shape=jax.ShapeDtypeStruct((b, s, A_HEADS * A_V_DIM), BF16),
        scratch_shapes=[pltpu.VMEM((2 * tq, LANE), BF16),
                        pltpu.VMEM((2 * tq, LANE), BF16),
                        pltpu.VMEM((s, LANE), BF16),
                        pltpu.VMEM((2 * tq, tk), F32),
                        pltpu.VMEM((2 * tq, tk), F32),
                        pltpu.VMEM((2 * tq, tk), F32),
                        pltpu.VMEM((2 * tq, LANE), F32),
                        pltpu.VMEM((2 * tq, 2 * LANE), F32)],
        compiler_params=pltpu.CompilerParams(
            dimension_semantics=("arbitrary", "arbitrary", "arbitrary"), vmem_limit_bytes=VMEM_LIMIT),
        name="diff_attn",
    )(p3, p3, p3, p3, p3, *tabs, *tabs, lq1, lk1, lq2, lk2, sub_g)


def _compress_kernel(xk_ref, xv_ref, pek_ref, pev_ref, w1k_ref, w2k_ref, w1v_ref, w2v_ref,
                     c_ref, s1_ref, s2_ref, kc_ref, vc_ref):
    n = xk_ref.shape[0]
    hd = B_HEAD_DIM
    half_w = CMP_STRIDE * hd
    tabs = (c_ref[...], s1_ref[...], s2_ref[...])

    def mlp(x_ref, g, pe_ref, w1_ref, w2_ref):
        x = jnp.concatenate([x_ref[:, (r * B_GROUPS + g) * hd:(r * B_GROUPS + g + 1) * hd]
                             for r in range(CMP_STRIDE)], axis=1).astype(F32)
        top = _dot((x + pe_ref[0:1, :]).astype(BF16), w1_ref[0:half_w, :])
        bot = _dot((x + pe_ref[1:2, :]).astype(BF16), w1_ref[half_w:2 * half_w, :])
        hid = top + pltpu.roll(bot, n - 1, axis=0)
        return _dot(_silu(hid).astype(BF16), w2_ref[...])

    for g in range(B_GROUPS):
        kc = mlp(xk_ref, g, pek_ref, w1k_ref, w2k_ref)
        kc_ref[g] = _rope(kc, tabs, hd // ROPE_FRACTION // 2).astype(kc_ref.dtype)
        vc_ref[g] = mlp(xv_ref, g, pev_ref, w1v_ref, w2v_ref).astype(vc_ref.dtype)


def _compress_call(xk, xv, pek, pev, w1k, w2k, w1v, w2v, tabs):
    b, n, xw = xk.shape
    x_spec = pl.BlockSpec((None, n, xw), lambda bi: (bi, 0, 0))
    o_spec = pl.BlockSpec((None, B_GROUPS, n, B_HEAD_DIM), lambda bi: (bi, 0, 0, 0))

    def full(a):
        return pl.BlockSpec(a.shape, lambda bi: (0,) * a.ndim)

    consts = (pek, pev, w1k, w2k, w1v, w2v, *tabs)
    return pl.pallas_call(
        _compress_kernel,
        grid=(b,),
        in_specs=[x_spec, x_spec] + [full(a) for a in consts],
        out_specs=[o_spec, o_spec],
        out_shape=[jax.ShapeDtypeStruct((b, B_GROUPS, n, B_HEAD_DIM), BF16)] * 2,
        compiler_params=pltpu.CompilerParams(
            dimension_semantics=("arbitrary",), vmem_limit_bytes=VMEM_LIMIT),
        name="compress",
    )(xk, xv, *consts)


def _select_blocks(imp_t, s0, n_slc, n_top):
    shape = imp_t.shape
    jb = lax.broadcasted_iota(jnp.int32, shape, 0)
    jbf = jb.astype(F32)
    qpos = s0 + lax.broadcasted_iota(jnp.int32, shape, 1)
    cur = qpos // SLC_BLOCK
    valid = (jb * SLC_BLOCK <= qpos) & (jb < n_slc)
    forced = (jb == 0) | (jb == cur) | (jb == cur - 1)
    imp = jnp.where(valid, jnp.where(forced, TAKEN, imp_t), NEG)
    for _ in range(n_top - N_FORCED):
        mx = jnp.max(imp, axis=0, keepdims=True)
        idx = jnp.min(jnp.where(imp == mx, jbf, float(LANE)), axis=0, keepdims=True)
        imp = jnp.where(jbf == idx, TAKEN, imp)
    return jnp.where((imp == TAKEN) & valid, 0.0, NEG)


def _nsa_kernel(q_ref, kc_ref, vc_ref, ks_ref, vs_ref, kw_ref, vw_ref, e_ref, ovt_ref, wbias_ref, gx_ref, gate_ref, z_ref,
                c_ref, s1_ref, s2_ref, perm_ref, o_ref, qa_ref, ksr_ref, kwr_ref, sa_ref, sb_ref, m_ref, acc_ref,
                *, tq, tk, n_slc, n_top):
    i = pl.program_id(2)
    s0 = i * tq
    hd = B_HEAD_DIM
    rows = B_HPG * tq
    half = hd // ROPE_FRACTION // 2

    sub = pl.multiple_of((i % (tk // tq)) * tq, tq)
    qtabs = tuple(r[pl.ds(sub, tq), :] for r in (c_ref, s1_ref, s2_ref))
    qh = [(_rope_mxu(q_ref[:, h * hd:(h + 1) * hd], qtabs, perm_ref[...]) * B_SCALE).astype(BF16)
          for h in range(B_HPG)]
    qs = jnp.concatenate(qh, axis=0)

    @pl.when(i % (tk // tq) == 0)
    def _():
        ktabs = (c_ref[...], s1_ref[...], s2_ref[...])
        k0 = pl.multiple_of(s0, tk)
        ksr_ref[pl.ds(k0, tk), :] = _rope_mxu(ks_ref[...], ktabs, perm_ref[...]).astype(BF16)
        kwr_ref[pl.ds(WINDOW + k0, tk), :] = _rope_mxu(kw_ref[...], ktabs, perm_ref[...]).astype(BF16)

    @pl.when(i == 0)
    def _():
        kwr_ref[0:WINDOW, :] = jnp.zeros((WINDOW, LANE), BF16)

    rid = lax.broadcasted_iota(jnp.int32, (rows, 1), 0)
    qpos_r = s0 + (rid & (tq - 1))

    n_cmp_pad = kc_ref.shape[0]
    cend = lax.broadcasted_iota(jnp.int32, (1, n_cmp_pad), 1) * CMP_STRIDE + (CMP_LEN - 1)
    sc = jnp.where(cend <= qpos_r, _dot_nt(qs, kc_ref[...]), NEG)
    pc = jnp.exp2(sc - jnp.max(sc, axis=-1, keepdims=True)).astype(BF16)
    pv = _dot(pc, _with_ones(vc_ref[...]))
    o_cmp = pv[:, 0:LANE] * jnp.where(qpos_r >= CMP_LEN - 1, 1.0 / pv[:, LANE:2 * LANE], 0.0)
    qpos_t = s0 + lax.broadcasted_iota(jnp.int32, (1, tq), 1)
    imp_t = jnp.zeros((LANE, tq), F32)
    for h in range(B_HPG):
        r = _dot_nt(ovt_ref[...], pc[h * tq:(h + 1) * tq])
        imp_t = imp_t + r[0:LANE] * jnp.where(qpos_t >= CMP_LEN - 1, 1.0 / r[LANE:LANE + 1], 0.0)

    kw_win = kwr_ref[pl.ds(pl.multiple_of(s0, tq), 2 * tq), :]
    s_win = [_dot_nt(qh[h], kw_win) for h in range(B_HPG)]

    selb16 =_select_blocks(imp_t, s0, n_slc, n_top).T.astype(BF16)
    for h in range(B_HPG):
        qa_ref[h * tq:(h + 1) * tq, 0:hd] = qh[h]
        qa_ref[h * tq:(h + 1) * tq, hd:2 * hd] = selb16
    _flash_init(m_ref, acc_ref)

    def slc_scores(j, s_ref):
        k0 = pl.multiple_of(j * tk, tk)
        ka = jnp.concatenate([ksr_ref[pl.ds(k0, tk), :], e_ref[pl.ds(k0, tk), :]], axis=1)
        s_ref[...] = _dot_nt(qa_ref[...], ka)

    def slc_update(j, s_ref, d):
        k0 = pl.multiple_of(j * tk, tk)
        s = s_ref[...]
        if d is not None:
            kpos = k0 + lax.broadcasted_iota(jnp.int32, (1, tk), 1)
            s = jnp.where(kpos <= qpos_r, s, NEG)
        _flash_update(s, vs_ref[pl.ds(k0, tk), :], m_ref, acc_ref)

    slc_scores(0, sa_ref)

    bias_lo = wbias_ref[jnp.where(i == 0, 2, 0)]
    bias_hi = wbias_ref[1]
    v_lo = vw_ref[pl.ds(pl.multiple_of(jnp.maximum(s0 - WINDOW, 0), tq), tq), :]
    v_win = _with_ones(jnp.concatenate([v_lo, vw_ref[pl.ds(pl.multiple_of(s0, tq), tq), :]], axis=0))
    o_win = []
    for h in range(B_HPG):
        sw = jnp.concatenate([s_win[h][:, 0:tq] + bias_lo, s_win[h][:, tq:2 * tq] + bias_hi], axis=1)
        pw = jnp.exp2(sw - jnp.max(sw, axis=-1, keepdims=True))
        pvw = _dot(pw.astype(BF16), v_win)
        o_win.append(pvw[:, 0:LANE] / pvw[:, LANE:2 * LANE])

    _run_tiles(0, s0 // tk, slc_scores, slc_update, (sa_ref, sb_ref))
    o_slc = _flash_result(acc_ref)

    gates = _dot(jax.nn.sigmoid(gate_ref[...].astype(F32)).astype(BF16), gx_ref[...])
    for h in range(B_HPG):
        r = slice(h * tq, (h + 1) * tq)
        g_cmp, g_slc, g_win = (gates[:, (3 * h + t) * LANE:(3 * h + t + 1) * LANE] for t in range(3))
        o = g_cmp * o_cmp[r] + g_slc * o_slc[r] + g_win * o_win[h]
        z = z_ref[:, h * hd:(h + 1) * hd].astype(F32)
        o_ref[:, h * hd:(h + 1) * hd] = (o * _silu(z)).astype(o_ref.dtype)


def _nsa_call(p3, kc, vc, e_mat, ovt_mat, wbias, gate_expand, tabs, perm):
    b, s, _ = p3.shape
    tq, tk = 512, 512
    assert tq == WINDOW
    n_slc = s // SLC_BLOCK
    n_top = min(SLC_TOPK, n_slc)
    gw = B_HPG * B_HEAD_DIM
    rows = B_HPG * tq
    n_cmp_pad = kc.shape[2]
    per_k = tk // tq

    def slab(cb):
        return pl.BlockSpec((None, s, LANE), lambda bi, g, i: (bi, 0, cb + g))

    def ktile(cb):
        return pl.BlockSpec((None, tk, LANE), lambda bi, g, i: (bi, i // per_k, cb + g))

    cmp_spec = pl.BlockSpec((None, None, n_cmp_pad, B_HEAD_DIM), lambda bi, g, i: (bi, g, 0, 0))
    tab_spec = pl.BlockSpec((tk, LANE), lambda bi, g, i: (i // per_k, 0))
    return pl.pallas_call(
        functools.partial(_nsa_kernel, tq=tq, tk=tk, n_slc=n_slc, n_top=n_top),
        grid=(b, B_GROUPS, s // tq),
        in_specs=[pl.BlockSpec((None, tq, gw), lambda bi, g, i: (bi, i, CB_BQ * LANE // gw + g)),
                  cmp_spec, cmp_spec,
                  ktile(CB_BKS), slab(CB_BVS), ktile(CB_BKW), slab(CB_BVW),
                  pl.BlockSpec((s, LANE), lambda bi, g, i: (0, 0), pipeline_mode=pl.Buffered(1)),
                  pl.BlockSpec((LANE + SUBLANE, n_cmp_pad), lambda bi, g, i: (0, 0), pipeline_mode=pl.Buffered(1)),
                  pl.BlockSpec(wbias.shape, lambda bi, g, i: (0, 0, 0), pipeline_mode=pl.Buffered(1)),
                  pl.BlockSpec(gate_expand.shape, lambda bi, g, i: (0, 0), pipeline_mode=pl.Buffered(1)),
                  pl.BlockSpec((None, tq, LANE), lambda bi, g, i: (bi, i, CB_BGATE + g)),
                  pl.BlockSpec((None, tq, gw), lambda bi, g, i: (bi, i, CB_BZ * LANE // gw + g)),
                  tab_spec, tab_spec, tab_spec,
                  pl.BlockSpec((LANE, LANE), lambda bi, g, i: (0, 0))],
        out_specs=pl.BlockSpec((None, tq, gw), lambda bi, g, i: (bi, i, g)),
        out_shape=jax.ShapeDtypeStruct((b, s, B_HEADS * B_HEAD_DIM), BF16),
        scratch_shapes=[pltpu.VMEM((rows, 2 * B_HEAD_DIM), BF16),
                        pltpu.VMEM((s, LANE), BF16),
                        pltpu.VMEM((s + WINDOW, LANE), BF16),
                        pltpu.VMEM((rows, tk), F32),
                        pltpu.VMEM((rows, tk), F32),
                        pltpu.VMEM((rows, LANE), F32),
                        pltpu.VMEM((rows, 2 * LANE), F32)],
        compiler_params=pltpu.CompilerParams(
            dimension_semantics=("arbitrary", "arbitrary", "arbitrary"), vmem_limit_bytes=VMEM_LIMIT),
        name="nsa",
    )(p3, kc, vc, p3, p3, p3, p3, e_mat, ovt_mat, wbias, gate_expand, p3, p3, *tabs, perm)


def _out_kernel(oa_ref, ob_ref, ga_ref, gb_ref, x_ref, gate_ref, wa_ref, wb_ref, wo_ref, fg_ref, o_ref):
    ya = _dot(oa_ref[...], wa_ref[...])
    yb = _dot(ob_ref[...], wb_ref[...])
    mix = (jax.nn.sigmoid(ga_ref[...].astype(F32)) * ya
           + jax.nn.sigmoid(gb_ref[...].astype(F32)) * yb)
    y = _dot(mix.astype(BF16), wo_ref[...])
    xo = x_ref[...] + gate_ref[...] * y
    o_ref[...] = xo * lax.rsqrt(jnp.mean(xo * xo, axis=-1, keepdims=True) + EPS) * fg_ref[...]


def _out_call(oa, ob, p2, x2, gate, wa, wb, wo, fg, seq):
    t, d = x2.shape
    tm = 256
    per_seq = seq // tm
    aw = oa.shape[1]
    bw = ob.shape[1]

    def resident(a):
        return pl.BlockSpec(a.shape, lambda i: (0, 0), pipeline_mode=pl.Buffered(1))

    return pl.pallas_call(
        _out_kernel,
        grid=(t // tm,),
        in_specs=[pl.BlockSpec((tm, aw), lambda i: (i, 0)),
                  pl.BlockSpec((tm, bw), lambda i: (i, 0)),
                  pl.BlockSpec((tm, d), lambda i: (i, CB_GA * LANE // d)),
                  pl.BlockSpec((tm, d), lambda i: (i, CB_GB * LANE // d)),
                  pl.BlockSpec((tm, d), lambda i: (i, 0)),
                  pl.BlockSpec((None, 1, d), lambda i: (i // per_seq, 0, 0)),
                  resident(wa), resident(wb), resident(wo),
                  pl.BlockSpec((1, d), lambda i: (0, 0))],
        out_specs=pl.BlockSpec((tm, d), lambda i: (i, 0)),
        out_shape=jax.ShapeDtypeStruct((t, d), F32),
        compiler_params=pltpu.CompilerParams(
            dimension_semantics=("arbitrary",), vmem_limit_bytes=VMEM_LIMIT),
        name="out_proj",
    )(oa, ob, p2, p2, x2, gate, wa, wb, wo, fg)


def _rope_tables(pos, head_dim):
    rd = head_dim // ROPE_FRACTION
    half = rd // 2
    inv = 1.0 / (ROPE_THETA ** (jnp.arange(half, dtype=F32) * (2.0 / rd)))
    ang = pos.astype(F32)[:, None] * inv[None, :]
    cos, sin = jnp.cos(ang), jnp.sin(ang)
    n = pos.shape[0]
    pad = jnp.zeros((n, head_dim - rd), F32)
    zero = jnp.zeros((n, half), F32)
    c = jnp.concatenate([cos, cos, pad + 1.0], axis=1)
    s1 = jnp.concatenate([-sin, zero, pad], axis=1)
    s2 = jnp.concatenate([zero, sin, pad], axis=1)
    reps = LANE // head_dim
    return tuple(jnp.tile(a, (1, reps)) for a in (c, s1, s2))


def _rope_perm(head_dim):
    half = head_dim // ROPE_FRACTION // 2
    src = jnp.arange(LANE)[:, None]
    dst = jnp.arange(LANE)[None, :]
    d = dst % head_dim
    first = (d < half) & (src == dst + half)
    second = (d >= half) & (d < 2 * half) & (src == dst - half)
    return (second.astype(F32) - first.astype(F32)).astype(BF16)


def _gate_weights(w):
    d = w.shape[0]
    gpg = B_HPG * 3
    a_g = W_MAIN_COLS
    a_m = a_g + B_GROUPS * gpg
    zpad = jnp.zeros((d, LANE - gpg), w.dtype)
    return jnp.concatenate([w[:, a_m:a_m + 2 * D_MODEL],
                            w[:, a_g:a_g + gpg], zpad, w[:, a_g + gpg:a_g + 2 * gpg], zpad,
                            jnp.zeros((d, 2 * LANE), w.dtype)], axis=1)


def kernel(x, c, w_ada, b_ada, norm_g, w_in, lambda_q1, lambda_k1, lambda_q2, lambda_k2, diff_norm_g,
           cmp_pe_k, cmp_pe_v, cmp_w1_k, cmp_w2_k, cmp_w1_v, cmp_w2_v, w_branch, w_out, final_norm_g):
    b, s, d = x.shape
    assert d == D_MODEL and s % 1024 == 0 and s // SLC_BLOCK <= LANE
    t = b * s
    x2 = x.reshape(t, d)

    c_pad = jnp.pad(c, ((0, 8 - b % 8 if b % 8 else 0), (0, 0)))
    mod = _ada_call(c_pad, w_ada[0], b_ada[0][None, :])[:b]
    shift = mod[:, None, 0:d]
    scale = mod[:, None, d:2 * d]
    gate = mod[:, None, 2 * d:3 * d]

    w_bf = w_in[0].astype(BF16)
    p2 = _inproj_call(x2, norm_g[0][None, :], scale, shift, _gate_weights(w_bf), w_bf, s)
    p3 = p2.reshape(b, s, P_WIDTH)

    pos = jnp.arange(s)
    oa = _diff_call(p3, _rope_tables(pos, A_QK_DIM), lambda_q1, lambda_k1, lambda_q2, lambda_k2, diff_norm_g)

    n_str = s // CMP_STRIDE
    hw = CMP_STRIDE * B_HEAD_DIM

    def strides(cb):
        return p3[:, :, cb * LANE:(cb + B_GROUPS) * LANE].reshape(b, n_str, CMP_STRIDE * B_GROUPS * B_HEAD_DIM)

    cmp_end = jnp.arange(n_str) * CMP_STRIDE + (CMP_LEN - 1)
    kc, vc = _compress_call(
        strides(CB_BKC), strides(CB_BVC),
        cmp_pe_k[0].reshape(2, hw), cmp_pe_v[0].reshape(2, hw),
        cmp_w1_k[0].astype(BF16), cmp_w2_k[0].astype(BF16),
        cmp_w1_v[0].astype(BF16), cmp_w2_v[0].astype(BF16),
        _rope_tables(cmp_end, B_HEAD_DIM))
    e_mat = (jnp.arange(s)[:, None] // SLC_BLOCK == jnp.arange(LANE)[None, :]).astype(BF16)
    cmp_start = jnp.arange(n_str) * CMP_STRIDE
    slc_start = jnp.arange(LANE) * SLC_BLOCK
    ovt_mat = ((cmp_start[None, :] < slc_start[:, None] + SLC_BLOCK)
               & (cmp_start[None, :] + CMP_LEN > slc_start[:, None])).astype(BF16)
    ovt_mat = jnp.concatenate([ovt_mat, jnp.ones((SUBLANE, n_str), BF16)], axis=0)
    tri = jnp.arange(WINDOW)[None, :] - jnp.arange(WINDOW)[:, None]
    wbias = jnp.stack([jnp.where(tri > 0, 0.0, NEG), jnp.where(tri <= 0, 0.0, NEG),
                       jnp.full((WINDOW, WINDOW), NEG)]).astype(F32)
    n_gate = 3 * B_HPG
    gate_expand = (jnp.arange(LANE)[:, None] == jnp.arange(n_gate * LANE)[None, :] // LANE).astype(BF16)
    ob = _nsa_call(p3, kc, vc, e_mat, ovt_mat, wbias, gate_expand, _rope_tables(pos, B_HEAD_DIM),
                   _rope_perm(B_HEAD_DIM))

    wbr = w_branch[0].astype(BF16)
    a_w = A_HEADS * A_V_DIM
    out = _out_call(oa.reshape(t, a_w), ob.reshape(t, B_HEADS * B_HEAD_DIM), p2, x2, gate,
                    wbr[:a_w], wbr[a_w:], w_out[0].astype(BF16), final_norm_g[None, :], s)
    return out.reshape(b, s, d)
```

```python
import functools
import math

import jax
import jax.numpy as jnp
from jax import lax
from jax.experimental import pallas as pl
from jax.experimental.pallas import tpu as pltpu

F32 = jnp.float32
BF16 = jnp.bfloat16

D_MODEL = 2048
A_HEADS = 8
A_QK_DIM = 64
A_V_DIM = 128
B_HEADS = 8
B_GROUPS = 2
B_HPG = B_HEADS // B_GROUPS
B_HEAD_DIM = 128
CMP_LEN = 32
CMP_STRIDE = 16
SLC_BLOCK = 64
SLC_TOPK = 16
N_FORCED = 3
WINDOW = 512
FORCE_SCORE = 1.0e4
ROPE_THETA = 500000.0
ROPE_FRACTION = 4
EPS = 1e-6
NEG = -2.0 ** 100
TAKEN = -2.0 ** 120
LAM_INIT = 0.8 - 0.6 * math.exp(-0.3 * 0)
LOG2E = math.log2(math.e)

LANE = 128
SUBLANE = 8
VMEM_LIMIT = 56 * 1024 * 1024

CB_GA, CB_GB, CB_BGATE, CB_MAIN = 0, 16, 32, 36
CB_AQ, CB_AK, CB_AV, CB_AZ, CB_BQ = (CB_MAIN + o for o in (0, 8, 16, 24, 32))
CB_BKC, CB_BVC, CB_BKS, CB_BVS, CB_BKW, CB_BVW, CB_BZ = (CB_MAIN + o for o in (40, 42, 44, 46, 48, 50, 52))
P_WIDTH = 96 * LANE
PROJ_TN = 1536
N_TAIL_TILES = (CB_MAIN * LANE) // PROJ_TN
W_MAIN_COLS = (CB_BZ + 8 - CB_MAIN) * LANE
A_SCALE = LOG2E * A_QK_DIM ** -0.5
B_SCALE = LOG2E * B_HEAD_DIM ** -0.5


def _dot(a, b):
    return jnp.dot(a, b, preferred_element_type=F32)


def _dot_nt(a, b):
    return lax.dot_general(a, b, (((1,), (1,)), ((), ())), preferred_element_type=F32)


def _silu(v):
    return v * jax.nn.sigmoid(v)


def _rope(x, tabs, half):
    c, s1, s2 = tabs
    return x * c + pltpu.roll(x, LANE - half, axis=1) * s1 + pltpu.roll(x, half, axis=1) * s2


def _rope_mxu(x16, tabs, perm):
    c, s1, s2 = tabs
    return x16.astype(F32) * c + _dot(x16, perm) * (s2 - s1)


def _ada_kernel(c_ref, w_ref, b_ref, o_ref):
    o_ref[...] = _dot(_silu(c_ref[...]), w_ref[...]) + b_ref[...]


def _ada_call(c_pad, w, b):
    rows, d = c_pad.shape
    n = w.shape[1]
    tn = 768
    return pl.pallas_call(
        _ada_kernel,
        grid=(n // tn,),
        in_specs=[pl.BlockSpec((rows, d), lambda j: (0, 0)),
                  pl.BlockSpec((d, tn), lambda j: (0, j)),
                  pl.BlockSpec((1, tn), lambda j: (0, j))],
        out_specs=pl.BlockSpec((rows, tn), lambda j: (0, j)),
        out_shape=jax.ShapeDtypeStruct((rows, n), F32),
        compiler_params=pltpu.CompilerParams(vmem_limit_bytes=VMEM_LIMIT),
        name="ada",
    )(c_pad, w, b)


def _inproj_kernel(x_ref, g_ref, sc_ref, sh_ref, wt_ref, wm_ref, o_ref, h_ref):
    j = pl.program_id(1)

    @pl.when(j == 0)
    def _():
        x = x_ref[...]
        ms = jnp.mean(x * x, axis=-1, keepdims=True)
        xn = x * lax.rsqrt(ms + EPS)
        h_ref[...] = (xn * (g_ref[...] * (1.0 + sc_ref[...])) + sh_ref[...]).astype(h_ref.dtype)

    @pl.when(j < N_TAIL_TILES)
    def _():
        o_ref[...] = _dot(h_ref[...], wt_ref[...]).astype(o_ref.dtype)

    @pl.when(j >= N_TAIL_TILES)
    def _():
        o_ref[...] = _dot(h_ref[...], wm_ref[...]).astype(o_ref.dtype)


def _inproj_call(x2, g, scale, shift, w_tail, w_main, seq):
    t, d = x2.shape
    tm = min(1024, seq)
    per_seq = seq // tm
    mod = pl.BlockSpec((None, 1, d), lambda i, j: (i // per_seq, 0, 0))
    return pl.pallas_call(
        _inproj_kernel,
        grid=(t // tm, P_WIDTH // PROJ_TN),
        in_specs=[pl.BlockSpec((tm, d), lambda i, j: (i, 0)),
                  pl.BlockSpec((1, d), lambda i, j: (0, 0)),
                  mod, mod,
                  pl.BlockSpec((d, PROJ_TN), lambda i, j: (0, jnp.minimum(j, N_TAIL_TILES - 1))),
                  pl.BlockSpec((d, PROJ_TN), lambda i, j: (0, jnp.maximum(j - N_TAIL_TILES, 0)))],
        out_specs=pl.BlockSpec((tm, PROJ_TN), lambda i, j: (i, j)),
        out_shape=jax.ShapeDtypeStruct((t, P_WIDTH), BF16),
        scratch_shapes=[pltpu.VMEM((tm, d), BF16)],
        compiler_params=pltpu.CompilerParams(
            dimension_semantics=("arbitrary", "arbitrary"), vmem_limit_bytes=VMEM_LIMIT),
        name="inproj",
    )(x2, g, scale, shift, w_tail, w_main)


def _flash_init(m_ref, acc_ref):
    m_ref[...] = jnp.full(m_ref.shape, NEG, F32)
    acc_ref[...] = jnp.zeros(acc_ref.shape, F32)


def _with_ones(v):
    return jnp.concatenate([v, jnp.ones(v.shape, v.dtype)], axis=1)


def _flash_update(s, v, m_ref, acc_ref, first=False):
    m_cur = jnp.max(s, axis=-1, keepdims=True)
    if first:
        m_new = jnp.broadcast_to(m_cur, m_ref.shape)
    else:
        m_old = m_ref[...]
        m_new = jnp.maximum(m_old, m_cur)
        alpha = jnp.exp2(m_old - m_new)
    p = jnp.exp2(s - jnp.tile(m_new, (1, s.shape[1] // LANE)))
    pv = _dot(p.astype(v.dtype), _with_ones(v))
    acc_ref[...] = pv if first else jnp.tile(alpha, (1, 2)) * acc_ref[...] + pv
    m_ref[...] = m_new


def _flash_result(acc_ref):
    acc = acc_ref[...]
    return acc[:, 0:LANE] / acc[:, LANE:2 * LANE]


def _run_tiles(first, last, scores, update, bufs, before_last=None, n_diag=1):
    n = last - first + 1
    nq = jnp.maximum((n - n_diag) // 4, 0)

    def quad(qi, carry):
        j = first + 4 * qi
        for k in range(4):
            scores(j + k + 1, bufs[(k + 1) % 2])
            update(j + k, bufs[k % 2], None)
        return carry

    lax.fori_loop(0, nq, quad, 0)
    base = first + 4 * nq
    rem = n - 4 * nq

    def tail(r):
        for k in range(r):
            if k + 1 < r:
                scores(base + k + 1, bufs[(k + 1) % 2])
            elif before_last is not None:
                before_last()
            d = n_diag - (r - k)
            update(base + k, bufs[k % 2], d if d >= 0 else None)

    for r in range(1, n_diag + 4):
        pl.when(rem == r)(functools.partial(tail, r))


def _diff_kernel(q_ref, qn_ref, k_ref, v_ref, z_ref, c_ref, s1_ref, s2_ref, cn_ref, s1n_ref, s2n_ref,
                 lq1_ref, lk1_ref, lq2_ref, lk2_ref, g_ref, o_ref,
                 qs_ref, qsn_ref, kr_ref, sa_ref, sb_ref, sc_ref, m_ref, acc_ref, *, tq, tk):
    i = pl.program_id(2)
    half = A_QK_DIM // ROPE_FRACTION // 2
    n_diag = tq // tk

    def build_qs(src_ref, tab_refs, dst_ref):
        q = _rope(src_ref[...].astype(F32), tuple(r[...] for r in tab_refs), half) * A_SCALE
        lane = lax.broadcasted_iota(jnp.int32, q.shape, 1)
        dst_ref[0:tq, :] = jnp.where(lane < A_QK_DIM, q, 0.0).astype(dst_ref.dtype)
        dst_ref[tq:2 * tq, :] = jnp.where(lane >= A_QK_DIM, q, 0.0).astype(dst_ref.dtype)

    def scores(j, s_ref):
        s_ref[...] = _dot_nt(qs_ref[...], kr_ref[pl.ds(pl.multiple_of(j * tk, tk), tk), :])

    def update(j, s_ref, d, first=False):
        s = s_ref[...]
        if d is not None:
            row = lax.broadcasted_iota(jnp.int32, s.shape, 0)
            col = lax.broadcasted_iota(jnp.int32, s.shape, 1)
            s = jnp.where(col + d * tk <= (row & (tq - 1)), s, NEG)
        _flash_update(s, v_ref[pl.ds(pl.multiple_of(j * tk, tk), tk), :], m_ref, acc_ref, first)

    def build_next():
        build_qs(qn_ref, (cn_ref, s1n_ref, s2n_ref), qsn_ref)

    def prefetch():
        sc_ref[...] = _dot_nt(qsn_ref[...], kr_ref[0:tk, :])

    def rope_keys():
        ktabs = (c_ref[...], s1_ref[...], s2_ref[...])
        kr_ref[pl.ds(pl.multiple_of(i * tq, tq), tq), :] = (
            _rope(k_ref[...].astype(F32), ktabs, half).astype(kr_ref.dtype))

    @pl.when(i == 0)
    def _():
        rope_keys()
        build_qs(q_ref, (c_ref, s1_ref, s2_ref), qs_ref)
        sc_ref[...] = _dot_nt(qs_ref[...], kr_ref[0:tk, :])
        for d in range(1, n_diag):
            scores(d, (sa_ref, sb_ref)[(d - 1) % 2])
        build_next()
        update(0, sc_ref, 0, True)
        prefetch()
        for d in range(1, n_diag):
            update(d, (sa_ref, sb_ref)[(d - 1) % 2], d)

    @pl.when(i > 0)
    def _():
        qs_ref[...] = qsn_ref[...]
        rope_keys()
        scores(1, sa_ref)
        build_next()
        update(0, sc_ref, None, True)
        _run_tiles(1, n_diag * (i + 1) - 1, scores, update, (sa_ref, sb_ref), before_last=prefetch, n_diag=n_diag)

    o = _flash_result(acc_ref)
    lam = (jnp.exp(jnp.sum(lq1_ref[...] * lk1_ref[...], axis=-1, keepdims=True))
           - jnp.exp(jnp.sum(lq2_ref[...] * lk2_ref[...], axis=-1, keepdims=True)) + LAM_INIT)
    d = o[0:tq] - lam * o[tq:2 * tq]
    dn = d * lax.rsqrt(jnp.mean(d * d, axis=-1, keepdims=True) + EPS)
    dn = dn * g_ref[...] * (1.0 - LAM_INIT)
    o_ref[...] = (dn * _silu(z_ref[...].astype(F32))).astype(o_ref.dtype)


def _diff_call(p3, tabs, lq1, lk1, lq2, lk2, sub_g):
    b, s, _ = p3.shape
    tq, tk = 512, 512
    last = s // tq - 1
    lam_spec = pl.BlockSpec((1, A_QK_DIM), lambda bi, h, i: (0, 0))
    tab_spec = pl.BlockSpec((tq, LANE), lambda bi, h, i: (i, 0))
    tabn_spec = pl.BlockSpec((tq, LANE), lambda bi, h, i: (jnp.minimum(i + 1, last), 0))
    return pl.pallas_call(
        functools.partial(_diff_kernel, tq=tq, tk=tk),
        grid=(b, A_HEADS, s // tq),
        in_specs=[pl.BlockSpec((None, tq, LANE), lambda bi, h, i: (bi, i, CB_AQ + h)),
                  pl.BlockSpec((None, tq, LANE), lambda bi, h, i: (bi, jnp.minimum(i + 1, last), CB_AQ + h)),
                  pl.BlockSpec((None, tq, LANE), lambda bi, h, i: (bi, i, CB_AK + h)),
                  pl.BlockSpec((None, s, LANE), lambda bi, h, i: (bi, 0, CB_AV + h)),
                  pl.BlockSpec((None, tq, LANE), lambda bi, h, i: (bi, i, CB_AZ + h)),
                  tab_spec, tab_spec, tab_spec, tabn_spec, tabn_spec, tabn_spec,
                  lam_spec, lam_spec, lam_spec, lam_spec,
                  pl.BlockSpec((1, A_V_DIM), lambda bi, h, i: (0, 0))],
        out_specs=pl.BlockSpec((None, tq, LANE), lambda bi, h, i: (bi, i, h)),
        out_shape=jax.ShapeDtypeStruct((b, s, A_HEADS * A_V_DIM), BF16),
        scratch_shapes=[pltpu.VMEM((2 * tq, LANE), BF16),
                        pltpu.VMEM((2 * tq, LANE), BF16),
                        pltpu.VMEM((s, LANE), BF16),
                        pltpu.VMEM((2 * tq, tk), F32),
                        pltpu.VMEM((2 * tq, tk), F32),
                        pltpu.VMEM((2 * tq, tk), F32),
                        pltpu.VMEM((2 * tq, LANE), F32),
                        pltpu.VMEM((2 * tq, 2 * LANE), F32)],
        compiler_params=pltpu.CompilerParams(
            dimension_semantics=("arbitrary", "arbitrary", "arbitrary"), vmem_limit_bytes=VMEM_LIMIT),
        name="diff_attn",
    )(p3, p3, p3, p3, p3, *tabs, *tabs, lq1, lk1, lq2, lk2, sub_g)


def _compress_kernel(xk_ref, xv_ref, pek_ref, pev_ref, w1k_ref, w2k_ref, w1v_ref, w2v_ref,
                     c_ref, s1_ref, s2_ref, kc_ref, vc_ref):
    n = xk_ref.shape[0]
    hd = B_HEAD_DIM
    half_w = CMP_STRIDE * hd
    tabs = (c_ref[...], s1_ref[...], s2_ref[...])

    def mlp(x_ref, g, pe_ref, w1_ref, w2_ref):
        x = jnp.concatenate([x_ref[:, (r * B_GROUPS + g) * hd:(r * B_GROUPS + g + 1) * hd]
                             for r in range(CMP_STRIDE)], axis=1).astype(F32)
        top = _dot((x + pe_ref[0:1, :]).astype(BF16), w1_ref[0:half_w, :])
        bot = _dot((x + pe_ref[1:2, :]).astype(BF16), w1_ref[half_w:2 * half_w, :])
        hid = top + pltpu.roll(bot, n - 1, axis=0)
        return _dot(_silu(hid).astype(BF16), w2_ref[...])

    for g in range(B_GROUPS):
        kc = mlp(xk_ref, g, pek_ref, w1k_ref, w2k_ref)
        kc_ref[g] = _rope(kc, tabs, hd // ROPE_FRACTION // 2).astype(kc_ref.dtype)
        vc_ref[g] = mlp(xv_ref, g, pev_ref, w1v_ref, w2v_ref).astype(vc_ref.dtype)


def _compress_call(xk, xv, pek, pev, w1k, w2k, w1v, w2v, tabs):
    b, n, xw = xk.shape
    x_spec = pl.BlockSpec((None, n, xw), lambda bi: (bi, 0, 0))
    o_spec = pl.BlockSpec((None, B_GROUPS, n, B_HEAD_DIM), lambda bi: (bi, 0, 0, 0))

    def full(a):
        return pl.BlockSpec(a.shape, lambda bi: (0,) * a.ndim)

    consts = (pek, pev, w1k, w2k, w1v, w2v, *tabs)
    return pl.pallas_call(
        _compress_kernel,
        grid=(b,),
        in_specs=[x_spec, x_spec] + [full(a) for a in consts],
        out_specs=[o_spec, o_spec],
        out_shape=[jax.ShapeDtypeStruct((b, B_GROUPS, n, B_HEAD_DIM), BF16)] * 2,
        compiler_params=pltpu.CompilerParams(
            dimension_semantics=("arbitrary",), vmem_limit_bytes=VMEM_LIMIT),
        name="compress",
    )(xk, xv, *consts)


def _select_blocks(imp_t, s0, n_slc, n_top):
    shape = imp_t.shape
    jb = lax.broadcasted_iota(jnp.int32, shape, 0)
    jbf = jb.astype(F32)
    qpos = s0 + lax.broadcasted_iota(jnp.int32, shape, 1)
    cur = qpos // SLC_BLOCK
    valid = (jb * SLC_BLOCK <= qpos) & (jb < n_slc)
    forced = (jb == 0) | (jb == cur) | (jb == cur - 1)
    imp = jnp.where(valid, jnp.where(forced, TAKEN, imp_t), NEG)
    for _ in range(n_top - N_FORCED):
        mx = jnp.max(imp, axis=0, keepdims=True)
        idx = jnp.min(jnp.where(imp == mx, jbf, float(LANE)), axis=0, keepdims=True)
        imp = jnp.where(jbf == idx, TAKEN, imp)
    return jnp.where((imp == TAKEN) & valid, 0.0, NEG)


def _nsa_kernel(q_ref, kc_ref, vc_ref, ks_ref, vs_ref, kw_ref, vw_ref, e_ref, ovt_ref, wbias_ref, gx_ref, gate_ref, z_ref,
                c_ref, s1_ref, s2_ref, perm_ref, o_ref, qa_ref, ksr_ref, kwr_ref, sa_ref, sb_ref, m_ref, acc_ref,
                *, tq, tk, n_slc, n_top):
    i = pl.program_id(2)
    s0 = i * tq
    hd = B_HEAD_DIM
    rows = B_HPG * tq

    tabs = (c_ref[...], s1_ref[...], s2_ref[...])
    qh = [(_rope_mxu(q_ref[:, h * hd:(h + 1) * hd], tabs, perm_ref[...]) * B_SCALE).astype(BF16)
          for h in range(B_HPG)]
    qs = jnp.concatenate(qh, axis=0)
    k0 = pl.multiple_of(s0, tk)
    ksr_ref[pl.ds(k0, tk), :] = _rope_mxu(ks_ref[...], tabs, perm_ref[...]).astype(BF16)
    kwr_ref[pl.ds(WINDOW + k0, tk), :] = _rope_mxu(kw_ref[...], tabs, perm_ref[...]).astype(BF16)

    @pl.when(i == 0)
    def _():
        kwr_ref[0:WINDOW, :] = jnp.zeros((WINDOW, LANE), BF16)

    rid = lax.broadcasted_iota(jnp.int32, (rows, 1), 0)
    qpos_r = s0 + (rid & (tq - 1))

    n_cmp_pad = kc_ref.shape[0]
    cend = lax.broadcasted_iota(jnp.int32, (1, n_cmp_pad), 1) * CMP_STRIDE + (CMP_LEN - 1)
    sc = jnp.where(cend <= qpos_r, _dot_nt(qs, kc_ref[...]), NEG)
    pc = jnp.exp2(sc - jnp.max(sc, axis=-1, keepdims=True)).astype(BF16)
    pv = _dot(pc, _with_ones(vc_ref[...]))
    o_cmp = pv[:, 0:LANE] * jnp.where(qpos_r >= CMP_LEN - 1, 1.0 / pv[:, LANE:2 * LANE], 0.0)
    qpos_t = s0 + lax.broadcasted_iota(jnp.int32, (1, tq), 1)
    imp_t = jnp.zeros((LANE, tq), F32)
    for h in range(B_HPG):
        r = _dot_nt(ovt_ref[...], pc[h * tq:(h + 1) * tq])
        imp_t = imp_t + r[0:LANE] * jnp.where(qpos_t >= CMP_LEN - 1, 1.0 / r[LANE:LANE + 1], 0.0)

    kw_win = kwr_ref[pl.ds(pl.multiple_of(s0, tq), 2 * tq), :]
    s_win = [_dot_nt(qh[h], kw_win) for h in range(B_HPG)]

    selb16 =_select_blocks(imp_t, s0, n_slc, n_top).T.astype(BF16)
    for h in range(B_HPG):
        qa_ref[h * tq:(h + 1) * tq, 0:hd] = qh[h]
        qa_ref[h * tq:(h + 1) * tq, hd:2 * hd] = selb16
    _flash_init(m_ref, acc_ref)

    def slc_scores(j, s_ref):
        k0 = pl.multiple_of(j * tk, tk)
        ka = jnp.concatenate([ksr_ref[pl.ds(k0, tk), :], e_ref[pl.ds(k0, tk), :]], axis=1)
        s_ref[...] = _dot_nt(qa_ref[...], ka)

    def slc_update(j, s_ref, d):
        k0 = pl.multiple_of(j * tk, tk)
        s = s_ref[...]
        if d is not None:
            kpos = k0 + lax.broadcasted_iota(jnp.int32, (1, tk), 1)
            s = jnp.where(kpos <= qpos_r, s, NEG)
        _flash_update(s, vs_ref[pl.ds(k0, tk), :], m_ref, acc_ref)

    slc_scores(0, sa_ref)

    bias_lo = wbias_ref[jnp.where(i == 0, 2, 0)]
    bias_hi = wbias_ref[1]
    v_lo = vw_ref[pl.ds(pl.multiple_of(jnp.maximum(s0 - WINDOW, 0), tq), tq), :]
    v_win = _with_ones(jnp.concatenate([v_lo, vw_ref[pl.ds(pl.multiple_of(s0, tq), tq), :]], axis=0))
    o_win = []
    for h in range(B_HPG):
        sw = jnp.concatenate([s_win[h][:, 0:tq] + bias_lo, s_win[h][:, tq:2 * tq] + bias_hi], axis=1)
        pw = jnp.exp2(sw - jnp.max(sw, axis=-1, keepdims=True))
        pvw = _dot(pw.astype(BF16), v_win)
        o_win.append(pvw[:, 0:LANE] / pvw[:, LANE:2 * LANE])

    _run_tiles(0, s0 // tk, slc_scores, slc_update, (sa_ref, sb_ref))

    gates = _dot(jax.nn.sigmoid(gate_ref[...].astype(F32)).astype(BF16), gx_ref[...])
    for h in range(B_HPG):
        r = slice(h * tq, (h + 1) * tq)
        g_cmp, g_slc, g_win = (gates[:, (3 * h + t) * LANE:(3 * h + t + 1) * LANE] for t in range(3))
        o_slc = acc_ref[r, 0:LANE] / acc_ref[r, LANE:2 * LANE]
        o = g_cmp * o_cmp[r] + g_slc * o_slc + g_win * o_win[h]
        z = z_ref[:, h * hd:(h + 1) * hd].astype(F32)
        o_ref[:, h * hd:(h + 1) * hd] = (o * _silu(z)).astype(o_ref.dtype)


def _nsa_call(p3, kc, vc, e_mat, ovt_mat, wbias, gate_expand, tabs, perm):
    b, s, _ = p3.shape
    tq, tk = 512, 512
    assert tq == tk == WINDOW
    n_slc = s // SLC_BLOCK
    n_top = min(SLC_TOPK, n_slc)
    gw = B_HPG * B_HEAD_DIM
    rows = B_HPG * tq
    n_cmp_pad = kc.shape[2]

    def slab(cb):
        return pl.BlockSpec((None, s, LANE), lambda bi, g, i: (bi, 0, cb + g))

    def ktile(cb):
        return pl.BlockSpec((None, tk, LANE), lambda bi, g, i: (bi, i, cb + g))

    cmp_spec = pl.BlockSpec((None, None, n_cmp_pad, B_HEAD_DIM), lambda bi, g, i: (bi, g, 0, 0))
    tab_spec = pl.BlockSpec((tk, LANE), lambda bi, g, i: (i, 0))
    return pl.pallas_call(
        functools.partial(_nsa_kernel, tq=tq, tk=tk, n_slc=n_slc, n_top=n_top),
        grid=(b, B_GROUPS, s // tq),
        in_specs=[pl.BlockSpec((None, tq, gw), lambda bi, g, i: (bi, i, CB_BQ * LANE // gw + g)),
                  cmp_spec, cmp_spec,
                  ktile(CB_BKS), slab(CB_BVS), ktile(CB_BKW), slab(CB_BVW),
                  pl.BlockSpec((s, LANE), lambda bi, g, i: (0, 0), pipeline_mode=pl.Buffered(1)),
                  pl.BlockSpec((LANE + SUBLANE, n_cmp_pad), lambda bi, g, i: (0, 0), pipeline_mode=pl.Buffered(1)),
                  pl.BlockSpec(wbias.shape, lambda bi, g, i: (0, 0, 0), pipeline_mode=pl.Buffered(1)),
                  pl.BlockSpec(gate_expand.shape, lambda bi, g, i: (0, 0), pipeline_mode=pl.Buffered(1)),
                  pl.BlockSpec((None, tq, LANE), lambda bi, g, i: (bi, i, CB_BGATE + g)),
                  pl.BlockSpec((None, tq, gw), lambda bi, g, i: (bi, i, CB_BZ * LANE // gw + g)),
                  tab_spec, tab_spec, tab_spec,
                  pl.BlockSpec((LANE, LANE), lambda bi, g, i: (0, 0))],
        out_specs=pl.BlockSpec((None, tq, gw), lambda bi, g, i: (bi, i, g)),
        out_shape=jax.ShapeDtypeStruct((b, s, B_HEADS * B_HEAD_DIM), BF16),
        scratch_shapes=[pltpu.VMEM((rows, 2 * B_HEAD_DIM), BF16),
                        pltpu.VMEM((s, LANE), BF16),
                        pltpu.VMEM((s + WINDOW, LANE), BF16),
                        pltpu.VMEM((rows, tk), F32),
                        pltpu.VMEM((rows, tk), F32),
                        pltpu.VMEM((rows, LANE), F32),
                        pltpu.VMEM((rows, 2 * LANE), F32)],
        compiler_params=pltpu.CompilerParams(
            dimension_semantics=("arbitrary", "arbitrary", "arbitrary"), vmem_limit_bytes=VMEM_LIMIT),
        name="nsa",
    )(p3, kc, vc, p3, p3, p3, p3, e_mat, ovt_mat, wbias, gate_expand, p3, p3, *tabs, perm)


def _out_kernel(oa_ref, ob_ref, ga_ref, gb_ref, x_ref, gate_ref, wa_ref, wb_ref, wo_ref, fg_ref, o_ref):
    ya = _dot(oa_ref[...], wa_ref[...])
    yb = _dot(ob_ref[...], wb_ref[...])
    mix = (jax.nn.sigmoid(ga_ref[...].astype(F32)) * ya
           + jax.nn.sigmoid(gb_ref[...].astype(F32)) * yb)
    y = _dot(mix.astype(BF16), wo_ref[...])
    xo = x_ref[...] + gate_ref[...] * y
    o_ref[...] = xo * lax.rsqrt(jnp.mean(xo * xo, axis=-1, keepdims=True) + EPS) * fg_ref[...]


def _out_call(oa, ob, p2, x2, gate, wa, wb, wo, fg, seq):
    t, d = x2.shape
    tm = 512
    per_seq = seq // tm
    aw = oa.shape[1]
    bw = ob.shape[1]

    def resident(a):
        return pl.BlockSpec(a.shape, lambda i: (0, 0), pipeline_mode=pl.Buffered(1))

    return pl.pallas_call(
        _out_kernel,
        grid=(t // tm,),
        in_specs=[pl.BlockSpec((tm, aw), lambda i: (i, 0)),
                  pl.BlockSpec((tm, bw), lambda i: (i, 0)),
                  pl.BlockSpec((tm, d), lambda i: (i, CB_GA * LANE // d)),
                  pl.BlockSpec((tm, d), lambda i: (i, CB_GB * LANE // d)),
                  pl.BlockSpec((tm, d), lambda i: (i, 0)),
                  pl.BlockSpec((None, 1, d), lambda i: (i // per_seq, 0, 0)),
                  resident(wa), resident(wb), resident(wo),
                  pl.BlockSpec((1, d), lambda i: (0, 0))],
        out_specs=pl.BlockSpec((tm, d), lambda i: (i, 0)),
        out_shape=jax.ShapeDtypeStruct((t, d), F32),
        compiler_params=pltpu.CompilerParams(
            dimension_semantics=("arbitrary",), vmem_limit_bytes=VMEM_LIMIT),
        name="out_proj",
    )(oa, ob, p2, p2, x2, gate, wa, wb, wo, fg)


def _rope_tables(pos, head_dim):
    rd = head_dim // ROPE_FRACTION
    half = rd // 2
    inv = 1.0 / (ROPE_THETA ** (jnp.arange(half, dtype=F32) * (2.0 / rd)))
    ang = pos.astype(F32)[:, None] * inv[None, :]
    cos, sin = jnp.cos(ang), jnp.sin(ang)
    n = pos.shape[0]
    pad = jnp.zeros((n, head_dim - rd), F32)
    zero = jnp.zeros((n, half), F32)
    c = jnp.concatenate([cos, cos, pad + 1.0], axis=1)
    s1 = jnp.concatenate([-sin, zero, pad], axis=1)
    s2 = jnp.concatenate([zero, sin, pad], axis=1)
    reps = LANE // head_dim
    return tuple(jnp.tile(a, (1, reps)) for a in (c, s1, s2))


def _rope_perm(head_dim):
    half = head_dim // ROPE_FRACTION // 2
    src = jnp.arange(LANE)[:, None]
    dst = jnp.arange(LANE)[None, :]
    d = dst % head_dim
    first = (d < half) & (src == dst + half)
    second = (d >= half) & (d < 2 * half) & (src == dst - half)
    return (second.astype(F32) - first.astype(F32)).astype(BF16)


def _gate_weights(w):
    d = w.shape[0]
    gpg = B_HPG * 3
    a_g = W_MAIN_COLS
    a_m = a_g + B_GROUPS * gpg
    zpad = jnp.zeros((d, LANE - gpg), w.dtype)
    return jnp.concatenate([w[:, a_m:a_m + 2 * D_MODEL],
                            w[:, a_g:a_g + gpg], zpad, w[:, a_g + gpg:a_g + 2 * gpg], zpad,
                            jnp.zeros((d, 2 * LANE), w.dtype)], axis=1)


def kernel(x, c, w_ada, b_ada, norm_g, w_in, lambda_q1, lambda_k1, lambda_q2, lambda_k2, diff_norm_g,
           cmp_pe_k, cmp_pe_v, cmp_w1_k, cmp_w2_k, cmp_w1_v, cmp_w2_v, w_branch, w_out, final_norm_g):
    b, s, d = x.shape
    assert d == D_MODEL and s % 1024 == 0 and s // SLC_BLOCK <= LANE
    t = b * s
    x2 = x.reshape(t, d)

    c_pad = jnp.pad(c, ((0, 8 - b % 8 if b % 8 else 0), (0, 0)))
    mod = _ada_call(c_pad, w_ada[0], b_ada[0][None, :])[:b]
    shift = mod[:, None, 0:d]
    scale = mod[:, None, d:2 * d]
    gate = mod[:, None, 2 * d:3 * d]

    w_bf = w_in[0].astype(BF16)
    p2 = _inproj_call(x2, norm_g[0][None, :], scale, shift, _gate_weights(w_bf), w_bf, s)
    p3 = p2.reshape(b, s, P_WIDTH)

    pos = jnp.arange(s)
    oa = _diff_call(p3, _rope_tables(pos, A_QK_DIM), lambda_q1, lambda_k1, lambda_q2, lambda_k2, diff_norm_g)

    n_str = s // CMP_STRIDE
    hw = CMP_STRIDE * B_HEAD_DIM

    def strides(cb):
        return p3[:, :, cb * LANE:(cb + B_GROUPS) * LANE].reshape(b, n_str, CMP_STRIDE * B_GROUPS * B_HEAD_DIM)

    cmp_end = jnp.arange(n_str) * CMP_STRIDE + (CMP_LEN - 1)
    kc, vc = _compress_call(
        strides(CB_BKC), strides(CB_BVC),
        cmp_pe_k[0].reshape(2, hw), cmp_pe_v[0].reshape(2, hw),
        cmp_w1_k[0].astype(BF16), cmp_w2_k[0].astype(BF16),
        cmp_w1_v[0].astype(BF16), cmp_w2_v[0].astype(BF16),
        _rope_tables(cmp_end, B_HEAD_DIM))
    e_mat = (jnp.arange(s)[:, None] // SLC_BLOCK == jnp.arange(LANE)[None, :]).astype(BF16)
    cmp_start = jnp.arange(n_str) * CMP_STRIDE
    slc_start = jnp.arange(LANE) * SLC_BLOCK
    ovt_mat = ((cmp_start[None, :] < slc_start[:, None] + SLC_BLOCK)
               & (cmp_start[None, :] + CMP_LEN > slc_start[:, None])).astype(BF16)
    ovt_mat = jnp.concatenate([ovt_mat, jnp.ones((SUBLANE, n_str), BF16)], axis=0)
    tri = jnp.arange(WINDOW)[None, :] - jnp.arange(WINDOW)[:, None]
    wbias = jnp.stack([jnp.where(tri > 0, 0.0, NEG), jnp.where(tri <= 0, 0.0, NEG),
                       jnp.full((WINDOW, WINDOW), NEG)]).astype(F32)
    n_gate = 3 * B_HPG
    gate_expand = (jnp.arange(LANE)[:, None] == jnp.arange(n_gate * LANE)[None, :] // LANE).astype(BF16)
    ob = _nsa_call(p3, kc, vc, e_mat, ovt_mat, wbias, gate_expand, _rope_tables(pos, B_HEAD_DIM),
                   _rope_perm(B_HEAD_DIM))

    wbr = w_branch[0].astype(BF16)
    a_w = A_HEADS * A_V_DIM
    out = _out_call(oa.reshape(t, a_w), ob.reshape(t, B_HEADS * B_HEAD_DIM), p2, x2, gate,
                    wbr[:a_w], wbr[a_w:], w_out[0].astype(BF16), final_norm_g[None, :], s)
    return out.reshape(b, s, d)
```
